```python
import jax, jax.numpy as jnp
from jax import lax
import numpy as np

D_MODEL = 1024
BATCH = 8
SEQ = 8192
DEPTH = 4
DEC_BATCH = 16
DEC_SEQ = 32
PAST_LEN = 2048

CHUNK = 64
N_MIXERS = 2
N_SGU_LAYERS = (DEPTH + 1) // 2
N_POOL_LAYERS = DEPTH // 2
D_SGU = 2 * D_MODEL
SGU_GROUPS = 8
SGU_GROUP_DIM = D_SGU // SGU_GROUPS
SGU_LEN = 128
POOL_WINDOWS = (2, 4, 8, 16)
POOL_GROUPS = len(POOL_WINDOWS)
POOL_GROUP_DIM = D_MODEL // POOL_GROUPS
POOL_MAX_W = max(POOL_WINDOWS)
POOL_STATE = POOL_MAX_W - 1
D_FF = 4 * D_MODEL
N_MOD = 6
EPS = 1e-6

kernel_name = "hybrid_sgu_pool_stream_encoder_step"


def rmsnorm(x, g):
    xf = x.astype(jnp.float32)
    y = xf * lax.rsqrt(jnp.mean(xf * xf, axis=-1, keepdims=True) + EPS)
    return (y * g.astype(jnp.float32)).astype(x.dtype)


def sgu_chunk_mask():
    pos = jnp.arange(SGU_LEN)
    return (pos[None, :] // CHUNK) <= (pos[:, None] // CHUNK)


def sgu_mixer(h, w_in, g_v, w_s, b_s, w_out, T):
    bsz, L, _ = h.shape
    z = jax.nn.gelu(h @ w_in)
    u, v = jnp.split(z, 2, axis=-1)
    v = rmsnorm(v, g_v)
    vr = v.reshape(bsz, L // T, T, SGU_GROUPS, SGU_GROUP_DIM)
    w = w_s[:, :T, :T] * sgu_chunk_mask()[:T, :T].astype(w_s.dtype)
    s = jnp.einsum('gij,bnjgd->bnigd', w, vr)
    s = s + jnp.transpose(b_s[:, :T])[None, None, :, :, None]
    s = s.reshape(bsz, L, D_SGU)
    return (u * s) @ w_out, v


def pool_mixer(h_ext, n_out, w_grp, scale):
    bsz, L, _ = h_ext.shape
    hf = h_ext.astype(jnp.float32)
    csum = jnp.pad(jnp.cumsum(hf, axis=1), ((0, 0), (POOL_MAX_W, 0), (0, 0)))
    rows = jnp.arange(L - n_out, L)
    start = POOL_MAX_W + L - n_out
    outs = []
    for g, w in enumerate(POOL_WINDOWS):
        sl = slice(g * POOL_GROUP_DIM, (g + 1) * POOL_GROUP_DIM)
        win = csum[:, start:start + n_out, sl] - csum[:, start - w:start - w + n_out, sl]
        cnt = jnp.minimum(w, rows + 1).astype(jnp.float32)[None, :, None]
        outs.append(win / cnt)
    pooled = jnp.concatenate(outs, axis=-1) - hf[:, L - n_out:]
    pooled = pooled.astype(h_ext.dtype).reshape(bsz, n_out, POOL_GROUPS, POOL_GROUP_DIM)
    y = jnp.einsum('btgc,gcd->btgd', pooled, w_grp).reshape(bsz, n_out, D_MODEL)
    return y * scale


def trunk(x, c, pool_cache, g_norm, w_ada, b_ada, sgu_w_in, sgu_g_v, sgu_w_s, sgu_b_s,
          sgu_w_out, pool_w_grp, pool_scale, ffn_w_up, ffn_w_down):
    sample = pool_cache is not None
    L = x.shape[1]
    sgu_states, pool_states = [], []
    for i in range(DEPTH):
        mod = jax.nn.silu(c) @ w_ada[i] + b_ada[i]
        sh1, sc1, gt1, sh2, sc2, gt2 = jnp.split(mod[:, None, :], N_MOD, axis=-1)
        h = rmsnorm(x, g_norm[i, 0]) * (1 + sc1) + sh1
        j = i // N_MIXERS
        if i % N_MIXERS == 0:
            T = L if sample else SGU_LEN
            mix, v = sgu_mixer(h, sgu_w_in[j], sgu_g_v[j], sgu_w_s[j], sgu_b_s[j], sgu_w_out[j], T)
            if sample:
                sgu_states.append(v)
        else:
            h_ext = jnp.concatenate([pool_cache[j].astype(h.dtype), h], axis=1) if sample else h
            mix = pool_mixer(h_ext, L, pool_w_grp[j], pool_scale[j])
            pool_states.append(h_ext[:, h_ext.shape[1] - POOL_STATE:])
        x = x + gt1 * rmsnorm(mix, g_norm[i, 1])
        h = rmsnorm(x, g_norm[i, 2]) * (1 + sc2) + sh2
        f = jnp.square(jax.nn.relu(h @ ffn_w_up[i])) @ ffn_w_down[i]
        x = x + gt2 * rmsnorm(f, g_norm[i, 3])
    return x, sgu_states, pool_states


def setup_inputs(seed: int = 0) -> dict:
    key = jax.random.key(seed)
    ks = jax.random.split(key, 20)
    f32 = jnp.float32
    nrm = lambda k, shape, s=1.0: (jax.random.normal(k, shape, f32) * s)
    return {
        "x_prompt": nrm(ks[0], (BATCH, SEQ, D_MODEL)),
        "x_sample": nrm(ks[1], (DEC_BATCH, DEC_SEQ, D_MODEL)),
        "cache_pool": nrm(ks[2], (N_POOL_LAYERS, DEC_BATCH, POOL_STATE, D_MODEL)),
        "c_prompt": nrm(ks[3], (BATCH, D_MODEL)),
        "c_sample": nrm(ks[4], (DEC_BATCH, D_MODEL)),
        "g_norm": 1.0 + nrm(ks[5], (DEPTH, 4, D_MODEL), 0.05),
        "w_ada": nrm(ks[6], (DEPTH, D_MODEL, N_MOD * D_MODEL), 0.5 * D_MODEL ** -0.5),
        "b_ada": nrm(ks[7], (DEPTH, N_MOD * D_MODEL), 0.02),
        "sgu_w_in": nrm(ks[8], (N_SGU_LAYERS, D_MODEL, 2 * D_SGU), D_MODEL ** -0.5),
        "sgu_g_v": 1.0 + nrm(ks[9], (N_SGU_LAYERS, D_SGU), 0.05),
        "sgu_w_s": nrm(ks[10], (N_SGU_LAYERS, SGU_GROUPS, SGU_LEN, SGU_LEN), SGU_LEN ** -0.5),
        "sgu_b_s": 1.0 + nrm(ks[11], (N_SGU_LAYERS, SGU_GROUPS, SGU_LEN), 0.1),
        "sgu_w_out": nrm(ks[12], (N_SGU_LAYERS, D_SGU, D_MODEL), D_SGU ** -0.5),
        "pool_w_grp": nrm(ks[13], (N_POOL_LAYERS, POOL_GROUPS, POOL_GROUP_DIM, POOL_GROUP_DIM), POOL_GROUP_DIM ** -0.5),
        "pool_scale": 1.0 + nrm(ks[14], (N_POOL_LAYERS, D_MODEL), 0.1),
        "ffn_w_up": nrm(ks[15], (DEPTH, D_MODEL, D_FF), D_MODEL ** -0.5),
        "ffn_w_down": nrm(ks[16], (DEPTH, D_FF, D_MODEL), D_FF ** -0.5),
    }


def reference(x_prompt, x_sample, cache_pool, c_prompt, c_sample, g_norm, w_ada, b_ada,
              sgu_w_in, sgu_g_v, sgu_w_s, sgu_b_s, sgu_w_out, pool_w_grp, pool_scale,
              ffn_w_up, ffn_w_down):
    y_prompt, _, pool_p = trunk(x_prompt, c_prompt, None, g_norm, w_ada, b_ada, sgu_w_in,
                                sgu_g_v, sgu_w_s, sgu_b_s, sgu_w_out, pool_w_grp, pool_scale,
                                ffn_w_up, ffn_w_down)
    y_sample, sgu_s, pool_s = trunk(x_sample, c_sample, cache_pool, g_norm, w_ada, b_ada, sgu_w_in,
                                    sgu_g_v, sgu_w_s, sgu_b_s, sgu_w_out, pool_w_grp, pool_scale,
                                    ffn_w_up, ffn_w_down)
    state_pool_prompt = jnp.stack(pool_p, axis=0)
    state_sgu_sample = jnp.stack(sgu_s, axis=0)
    state_pool_sample = jnp.stack(pool_s, axis=0)
    return (y_prompt, y_sample, state_pool_prompt, state_sgu_sample, state_pool_sample)
```

```python
import functools

import jax
import jax.numpy as jnp
from jax import lax
from jax.experimental import pallas as pl
from jax.experimental.pallas import tpu as pltpu

D_MODEL = 1024
DEPTH = 4
N_MOD = 6
EPS = 1e-6
D_SGU = 2 * D_MODEL
SGU_GROUPS = 8
SGU_GROUP_DIM = D_SGU // SGU_GROUPS
SGU_LEN = 128
CHUNK = 64
POOL_WINDOWS = (2, 4, 8, 16)
POOL_GROUP_DIM = D_MODEL // len(POOL_WINDOWS)
POOL_HALO = max(POOL_WINDOWS)
POOL_STATE = POOL_HALO - 1
D_FF = 4 * D_MODEL

TILE_TOKENS = 512
FF_CHUNK = 512
MOD_COLS = 1536
MOD_ROWS = 32
VMEM_LIMIT_BYTES = 56 * 1024 * 1024

_F32 = jnp.float32
_BF16 = jnp.bfloat16


def _dot(a, b):
    return jnp.dot(a, b, preferred_element_type=_F32)


def _rms(x, g):
    return x * lax.rsqrt(jnp.mean(x * x, axis=-1, keepdims=True) + EPS) * g


def _gelu_tanh(x):
    c = 0.7978845608028654
    return 0.5 * x * (1.0 + jnp.tanh(c * (x + 0.044715 * (x * x * x))))


def _mod_kernel(c_ref, w_ref, b_ref, o_ref):
    c = c_ref[...]
    s = (c * jax.nn.sigmoid(c)).astype(_BF16)
    o_ref[0] = _dot(s, w_ref[0].astype(_BF16)) + b_ref[0]


def _modulation(c_all, w_ada, b_ada):
    n_col = N_MOD * D_MODEL
    return pl.pallas_call(
        _mod_kernel,
        grid=(DEPTH, n_col // MOD_COLS),
        in_specs=[
            pl.BlockSpec((MOD_ROWS, D_MODEL), lambda i, j: (0, 0)),
            pl.BlockSpec((1, D_MODEL, MOD_COLS), lambda i, j: (i, 0, j)),
            pl.BlockSpec((1, 1, MOD_COLS), lambda i, j: (i, 0, j)),
        ],
        out_specs=pl.BlockSpec((1, MOD_ROWS, MOD_COLS), lambda i, j: (i, 0, j)),
        out_shape=jax.ShapeDtypeStruct((DEPTH, MOD_ROWS, n_col), _F32),
        compiler_params=pltpu.CompilerParams(
            dimension_semantics=("arbitrary", "arbitrary"),
            vmem_limit_bytes=VMEM_LIMIT_BYTES),
        name="adaln_mod",
    )(c_all, w_ada, b_ada.reshape(DEPTH, 1, n_col))


def _sgu_kernel(x_ref, mod_ref, g_ref, w_in_ref, gv_ref, ws_ref, mask_ref, bs_ref,
                w_out_ref, o_ref, *v_ref, streams, rows):
    m = streams * rows
    x = x_ref[...]
    mod = mod_ref[...]
    sh1, sc1, gt1 = mod[:, 0:1], mod[:, 1:2], mod[:, 2:3]
    g = g_ref[...]
    h = _rms(x, g[0:1]) * (1.0 + sc1) + sh1
    hb = h.reshape(m, D_MODEL).astype(_BF16)

    v = _gelu_tanh(_dot(hb, w_in_ref[:, D_SGU:]))
    v = _rms(v, gv_ref[...])
    if v_ref:
        v_ref[0][...] = v.reshape(streams, rows, D_SGU)
    vb = v.astype(_BF16)

    mask = mask_ref[...]
    acc = jnp.zeros((m, D_MODEL), _F32)
    for grp in range(SGU_GROUPS):
        cols = slice(grp * SGU_GROUP_DIM, (grp + 1) * SGU_GROUP_DIM)
        u = _gelu_tanh(_dot(hb, w_in_ref[:, cols]))
        w = (ws_ref[grp] * mask).astype(_BF16)
        bias = bs_ref[grp]
        s = [_dot(w, vb[n * SGU_LEN:(n + 1) * SGU_LEN, cols]) + bias
             for n in range(m // SGU_LEN)]
        s = jnp.concatenate(s, axis=0) if len(s) > 1 else s[0]
        acc = acc + _dot((u * s).astype(_BF16), w_out_ref[cols, :])

    mix = _rms(acc, g[1:2]).reshape(streams, rows, D_MODEL)
    o_ref[...] = x + gt1 * mix


def _sgu_layer(x, mod, g_norm, w_in, g_v, w_s, mask, b_s, w_out, *, streams, rows,
               emit_v):
    bsz, seq, _ = x.shape
    grid = (bsz // streams, seq // rows)
    tile = lambda d: pl.BlockSpec((streams, rows, d), lambda b, t: (b, t, 0))
    full = lambda a: pl.BlockSpec(a.shape, lambda b, t: (0,) * a.ndim)
    out_shape = [jax.ShapeDtypeStruct(x.shape, _F32)]
    out_specs = [tile(D_MODEL)]
    if emit_v:
        out_shape.append(jax.ShapeDtypeStruct((bsz, seq, D_SGU), _F32))
        out_specs.append(tile(D_SGU))
    return pl.pallas_call(
        functools.partial(_sgu_kernel, streams=streams, rows=rows),
        grid=grid,
        in_specs=[
            tile(D_MODEL),
            pl.BlockSpec((streams, N_MOD, D_MODEL), lambda b, t: (b, 0, 0)),
            full(g_norm), full(w_in), full(g_v), full(w_s), full(mask), full(b_s),
            full(w_out),
        ],
        out_specs=out_specs,
        out_shape=out_shape,
        compiler_params=pltpu.CompilerParams(
            dimension_semantics=("arbitrary", "arbitrary"),
            vmem_limit_bytes=VMEM_LIMIT_BYTES),
        name="sgu_mixer",
    )(x, mod, g_norm, w_in, g_v, w_s, mask, b_s, w_out)


def _pool_kernel(x_ref, halo_ref, mod_ref, g_ref, w_ref, scale_ref, o_ref, tail_ref,
                 ext_ref, *, streams, rows, hist, halo_is_h):
    t = pl.program_id(1)
    x = x_ref[...]
    mod = mod_ref[...]
    sh1, sc1, gt1 = mod[:, 0:1], mod[:, 1:2], mod[:, 2:3]
    g = g_ref[...]
    h = _rms(x, g[0:1]) * (1.0 + sc1) + sh1

    halo = halo_ref[...]
    if not halo_is_h:
        halo = _rms(halo, g[0:1]) * (1.0 + sc1) + sh1
        halo = jnp.where(t > 0, halo, 0.0)
    ext_ref[:, 0:POOL_HALO, :] = halo
    ext_ref[:, POOL_HALO:, :] = h
    tail_ref[...] = ext_ref[:, rows:, :]

    pos = lax.broadcasted_iota(jnp.int32, (1, rows, 1), 1) + (t * rows + hist)
    outs = []
    for grp, win in enumerate(POOL_WINDOWS):
        cols = slice(grp * POOL_GROUP_DIM, (grp + 1) * POOL_GROUP_DIM)
        tot = h[:, :, cols]
        for k in range(1, win):
            tot = tot + ext_ref[:, POOL_HALO - k:POOL_HALO - k + rows, cols]
        cnt = jnp.minimum(win, pos + 1).astype(_F32)
        pooled = (tot / cnt - h[:, :, cols]).reshape(streams * rows, POOL_GROUP_DIM)
        outs.append(_dot(pooled.astype(_BF16), w_ref[grp]))
    y = jnp.concatenate(outs, axis=-1) * scale_ref[...]
    mix = _rms(y, g[1:2]).reshape(streams, rows, D_MODEL)
    o_ref[...] = x + gt1 * mix


def _pool_layer(x, halo_src, mod, g_norm, w_grp, scale, *, streams, rows, hist,
                halo_is_h):
    bsz, seq, _ = x.shape
    grid = (bsz // streams, seq // rows)
    tile = pl.BlockSpec((streams, rows, D_MODEL), lambda b, t: (b, t, 0))
    full = lambda a: pl.BlockSpec(a.shape, lambda b, t: (0,) * a.ndim)
    halo_blocks = rows // POOL_HALO
    if halo_is_h:
        halo_spec = pl.BlockSpec((streams, POOL_HALO, D_MODEL), lambda b, t: (b, 0, 0))
    else:
        halo_spec = pl.BlockSpec(
            (streams, POOL_HALO, D_MODEL),
            lambda b, t: (b, jnp.maximum(t * halo_blocks - 1, 0), 0))
    return pl.pallas_call(
        functools.partial(_pool_kernel, streams=streams, rows=rows, hist=hist,
                          halo_is_h=halo_is_h),
        grid=grid,
        in_specs=[
            tile, halo_spec,
            pl.BlockSpec((streams, N_MOD, D_MODEL), lambda b, t: (b, 0, 0)),
            full(g_norm), full(w_grp), full(scale),
        ],
        out_specs=[
            tile,
            pl.BlockSpec((streams, POOL_HALO, D_MODEL), lambda b, t: (b, 0, 0)),
        ],
        out_shape=[
            jax.ShapeDtypeStruct(x.shape, _F32),
            jax.ShapeDtypeStruct((bsz, POOL_HALO, D_MODEL), _F32),
        ],
        scratch_shapes=[pltpu.VMEM((streams, rows + POOL_HALO, D_MODEL), _F32)],
        compiler_params=pltpu.CompilerParams(
            dimension_semantics=("arbitrary", "arbitrary"),
            vmem_limit_bytes=VMEM_LIMIT_BYTES),
        name="pool_mixer",
    )(x, halo_src, mod, g_norm, w_grp, scale)


def _ffn_kernel(x_ref, mod_ref, g_ref, w_up_ref, w_down_ref, o_ref, *, streams, rows):
    m = streams * rows
    x = x_ref[...]
    mod = mod_ref[...]
    sh2, sc2, gt2 = mod[:, 3:4], mod[:, 4:5], mod[:, 5:6]
    g = g_ref[...]
    h = _rms(x, g[2:3]) * (1.0 + sc2) + sh2
    hb = h.reshape(m, D_MODEL).astype(_BF16)
    acc = jnp.zeros((m, D_MODEL), _F32)
    for c in range(D_FF // FF_CHUNK):
        cols = slice(c * FF_CHUNK, (c + 1) * FF_CHUNK)
        a = jnp.maximum(_dot(hb, w_up_ref[:, cols]), 0.0)
        acc = acc + _dot((a * a).astype(_BF16), w_down_ref[cols, :])
    f = _rms(acc, g[3:4]).reshape(streams, rows, D_MODEL)
    o_ref[...] = x + gt2 * f


def _ffn_layer(x, mod, g_norm, w_up, w_down, *, streams, rows):
    bsz, seq, _ = x.shape
    grid = (bsz // streams, seq // rows)
    tile = pl.BlockSpec((streams, rows, D_MODEL), lambda b, t: (b, t, 0))
    full = lambda a: pl.BlockSpec(a.shape, lambda b, t: (0,) * a.ndim)
    return pl.pallas_call(
        functools.partial(_ffn_kernel, streams=streams, rows=rows),
        grid=grid,
        in_specs=[
            tile,
            pl.BlockSpec((streams, N_MOD, D_MODEL), lambda b, t: (b, 0, 0)),
            full(g_norm), full(w_up), full(w_down),
        ],
        out_specs=tile,
        out_shape=jax.ShapeDtypeStruct(x.shape, _F32),
        compiler_params=pltpu.CompilerParams(
            dimension_semantics=("arbitrary", "arbitrary"),
            vmem_limit_bytes=VMEM_LIMIT_BYTES),
        name="channel_mlp",
    )(x, mod, g_norm, w_up, w_down)


def _sgu_mask(length):
    pos = jnp.arange(SGU_LEN)
    same_chunk = (pos[:, None] // length) == (pos[None, :] // length)
    causal = (pos[None, :] % length) // CHUNK <= (pos[:, None] % length) // CHUNK
    return (same_chunk & causal).astype(_F32)


def _trunk(x, mod, pool_cache, weights, *, streams, rows):
    (g_norm, sgu_w_in, sgu_g_v, sgu_w_s, sgu_b_s, sgu_w_out, pool_w_grp, pool_scale,
     ffn_w_up, ffn_w_down) = weights
    sample = pool_cache is not None
    seq = x.shape[1]
    length = seq if sample else SGU_LEN
    reps = SGU_LEN // length
    mask = _sgu_mask(length)
    sgu_states, pool_states = [], []
    for i in range(DEPTH):
        j = i // 2
        if i % 2 == 0:
            w_s = jnp.tile(sgu_w_s[j][:, :length, :length], (1, reps, reps))
            b_s = jnp.tile(sgu_b_s[j][:, :length], (1, reps))[:, :, None]
            outs = _sgu_layer(x, mod[i], g_norm[i], sgu_w_in[j], sgu_g_v[j][None, :],
                              w_s, mask, b_s, sgu_w_out[j], streams=streams, rows=rows,
                              emit_v=sample)
            x = outs[0]
            if sample:
                sgu_states.append(outs[1])
        else:
            if sample:
                halo = jnp.pad(pool_cache[j], ((0, 0), (1, 0), (0, 0)))
            else:
                halo = x
            x, tail = _pool_layer(x, halo, mod[i], g_norm[i], pool_w_grp[j],
                                  pool_scale[j][None, :], streams=streams, rows=rows,
                                  hist=POOL_STATE if sample else 0, halo_is_h=sample)
            pool_states.append(tail[:, 1:, :])
        x = _ffn_layer(x, mod[i], g_norm[i], ffn_w_up[i], ffn_w_down[i],
                       streams=streams, rows=rows)
    return x, sgu_states, pool_states


def kernel(x_prompt, x_sample, cache_pool, c_prompt, c_sample, g_norm, w_ada, b_ada,
           sgu_w_in, sgu_g_v, sgu_w_s, sgu_b_s, sgu_w_out, pool_w_grp, pool_scale,
           ffn_w_up, ffn_w_down):
    n_p, n_s = c_prompt.shape[0], c_sample.shape[0]
    c_all = jnp.concatenate(
        [c_prompt, c_sample, jnp.zeros((MOD_ROWS - n_p - n_s, D_MODEL), _F32)], axis=0)
    mod = _modulation(c_all, w_ada, b_ada)
    mod = mod.reshape(DEPTH, MOD_ROWS, N_MOD, D_MODEL)
    mod_p, mod_s = mod[:, :n_p], mod[:, n_p:n_p + n_s]

    weights = (g_norm, sgu_w_in.astype(_BF16), sgu_g_v, sgu_w_s, sgu_b_s,
               sgu_w_out.astype(_BF16), pool_w_grp.astype(_BF16), pool_scale,
               ffn_w_up.astype(_BF16), ffn_w_down.astype(_BF16))

    y_p, _, pool_p = _trunk(x_prompt, mod_p, None, weights,
                            streams=1, rows=TILE_TOKENS)
    s_rows = x_sample.shape[1]
    y_s, sgu_s, pool_s = _trunk(x_sample, mod_s, cache_pool, weights,
                                streams=TILE_TOKENS // s_rows, rows=s_rows)
    return (y_p, y_s, jnp.stack(pool_p, axis=0), jnp.stack(sgu_s, axis=0),
            jnp.stack(pool_s, axis=0))
```

```python
import functools

import jax
import jax.numpy as jnp
from jax import lax
from jax.experimental import pallas as pl
from jax.experimental.pallas import tpu as pltpu

D_MODEL = 1024
DEPTH = 4
N_MOD = 6
EPS = 1e-6
D_SGU = 2 * D_MODEL
SGU_GROUPS = 8
SGU_GROUP_DIM = D_SGU // SGU_GROUPS
SGU_LEN = 128
SGU_PAIR = 2
CHUNK = 64
POOL_WINDOWS = (2, 4, 8, 16)
POOL_GROUP_DIM = D_MODEL // len(POOL_WINDOWS)
POOL_HALO = max(POOL_WINDOWS)
POOL_STATE = POOL_HALO - 1
D_FF = 4 * D_MODEL

TILE_TOKENS = 512
FF_CHUNK = 512
MOD_COLS = 1536
MOD_ROWS = 32
VMEM_LIMIT_BYTES = 56 * 1024 * 1024

_F32 = jnp.float32
_BF16 = jnp.bfloat16


def _dot(a, b):
    return jnp.dot(a, b, preferred_element_type=_F32)


def _rms(x, g):
    return x * lax.rsqrt(jnp.mean(x * x, axis=-1, keepdims=True) + EPS) * g


def _gelu_tanh(x):
    c = 0.7978845608028654
    half = 0.5 * x
    return half + half * jnp.tanh(x * (c + (c * 0.044715) * (x * x)))


def _mod_kernel(c_ref, w_ref, b_ref, o_ref):
    c = c_ref[...]
    s = (c * jax.nn.sigmoid(c)).astype(_BF16)
    o_ref[0] = _dot(s, w_ref[0].astype(_BF16)) + b_ref[0]


def _modulation(c_all, w_ada, b_ada):
    n_col = N_MOD * D_MODEL
    return pl.pallas_call(
        _mod_kernel,
        grid=(DEPTH, n_col // MOD_COLS),
        in_specs=[
            pl.BlockSpec((MOD_ROWS, D_MODEL), lambda i, j: (0, 0)),
            pl.BlockSpec((1, D_MODEL, MOD_COLS), lambda i, j: (i, 0, j)),
            pl.BlockSpec((1, 1, MOD_COLS), lambda i, j: (i, 0, j)),
        ],
        out_specs=pl.BlockSpec((1, MOD_ROWS, MOD_COLS), lambda i, j: (i, 0, j)),
        out_shape=jax.ShapeDtypeStruct((DEPTH, MOD_ROWS, n_col), _F32),
        compiler_params=pltpu.CompilerParams(
            dimension_semantics=("arbitrary", "arbitrary"),
            vmem_limit_bytes=VMEM_LIMIT_BYTES),
        name="adaln_mod",
    )(c_all, w_ada, b_ada.reshape(DEPTH, 1, n_col))


def _sgu_kernel(x_ref, mod_ref, g_ref, w_in_ref, gv_ref, ws_ref, mask_ref, bs_ref,
                w_out_ref, o_ref, *v_ref, streams, rows):
    m = streams * rows
    x = x_ref[...]
    mod = mod_ref[...]
    sh1, sc1, gt1 = mod[:, 0:1], mod[:, 1:2], mod[:, 2:3]
    g = g_ref[...]
    h = _rms(x, g[0:1]) * (1.0 + sc1) + sh1
    hb = h.reshape(m, D_MODEL).astype(_BF16)

    v = _gelu_tanh(_dot(hb, w_in_ref[:, D_SGU:]))
    v = _rms(v, gv_ref[...])
    if v_ref:
        v_ref[0][...] = v.reshape(streams, rows, D_SGU)
    vb = v.astype(_BF16)

    mask = mask_ref[...]
    n_chunks = m // SGU_LEN
    acc = jnp.zeros((m, D_MODEL), _F32)
    for pair in range(SGU_GROUPS // SGU_PAIR):
        pcols = slice(pair * SGU_PAIR * SGU_GROUP_DIM, (pair + 1) * SGU_PAIR * SGU_GROUP_DIM)
        u = _gelu_tanh(_dot(hb, w_in_ref[:, pcols]))
        s_pair = []
        for grp in range(pair * SGU_PAIR, (pair + 1) * SGU_PAIR):
            cols = slice(grp * SGU_GROUP_DIM, (grp + 1) * SGU_GROUP_DIM)
            w = (ws_ref[grp] * mask).astype(_BF16)
            v_wide = jnp.concatenate(
                [vb[n * SGU_LEN:(n + 1) * SGU_LEN, cols] for n in range(n_chunks)], axis=1)
            s_wide = _dot(w, v_wide) + bs_ref[grp]
            s_pair.append(jnp.concatenate(
                [s_wide[:, n * SGU_GROUP_DIM:(n + 1) * SGU_GROUP_DIM]
                 for n in range(n_chunks)], axis=0))
        s = jnp.concatenate(s_pair, axis=1)
        acc = acc + _dot((u * s).astype(_BF16), w_out_ref[pcols, :])

    mix = _rms(acc, g[1:2]).reshape(streams, rows, D_MODEL)
    o_ref[...] = x + gt1 * mix


def _sgu_layer(x, mod, g_norm, w_in, g_v, w_s, mask, b_s, w_out, *, streams, rows,
               emit_v):
    bsz, seq, _ = x.shape
    grid = (bsz // streams, seq // rows)
    tile = lambda d: pl.BlockSpec((streams, rows, d), lambda b, t: (b, t, 0))
    full = lambda a: pl.BlockSpec(a.shape, lambda b, t: (0,) * a.ndim)
    out_shape = [jax.ShapeDtypeStruct(x.shape, _F32)]
    out_specs = [tile(D_MODEL)]
    if emit_v:
        out_shape.append(jax.ShapeDtypeStruct((bsz, seq, D_SGU), _F32))
        out_specs.append(tile(D_SGU))
    return pl.pallas_call(
        functools.partial(_sgu_kernel, streams=streams, rows=rows),
        grid=grid,
        in_specs=[
            tile(D_MODEL),
            pl.BlockSpec((streams, N_MOD, D_MODEL), lambda b, t: (b, 0, 0)),
            full(g_norm), full(w_in), full(g_v), full(w_s), full(mask), full(b_s),
            full(w_out),
        ],
        out_specs=out_specs,
        out_shape=out_shape,
        compiler_params=pltpu.CompilerParams(
            dimension_semantics=("arbitrary", "arbitrary"),
            vmem_limit_bytes=VMEM_LIMIT_BYTES),
        name="sgu_mixer",
    )(x, mod, g_norm, w_in, g_v, w_s, mask, b_s, w_out)


def _pool_kernel(x_ref, halo_ref, mod_ref, g_ref, w_ref, scale_ref, o_ref, tail_ref,
                 ext_ref, *, streams, rows, hist, halo_is_h):
    t = pl.program_id(1)
    x = x_ref[...]
    mod = mod_ref[...]
    sh1, sc1, gt1 = mod[:, 0:1], mod[:, 1:2], mod[:, 2:3]
    g = g_ref[...]
    h = _rms(x, g[0:1]) * (1.0 + sc1) + sh1

    halo = halo_ref[...]
    if not halo_is_h:
        halo = _rms(halo, g[0:1]) * (1.0 + sc1) + sh1
        halo = jnp.where(t > 0, halo, 0.0)
    ext_ref[:, 0:POOL_HALO, :] = halo
    ext_ref[:, POOL_HALO:, :] = h
    tail_ref[...] = ext_ref[:, rows:, :]

    pos = lax.broadcasted_iota(jnp.int32, (1, rows, 1), 1) + (t * rows + hist)
    outs = []
    for grp, win in enumerate(POOL_WINDOWS):
        cols = slice(grp * POOL_GROUP_DIM, (grp + 1) * POOL_GROUP_DIM)
        tot = h[:, :, cols]
        for k in range(1, win):
            tot = tot + ext_ref[:, POOL_HALO - k:POOL_HALO - k + rows, cols]
        cnt = jnp.minimum(win, pos + 1).astype(_F32)
        pooled = (tot / cnt - h[:, :, cols]).reshape(streams * rows, POOL_GROUP_DIM)
        outs.append(_dot(pooled.astype(_BF16), w_ref[grp]))
    y = jnp.concatenate(outs, axis=-1) * scale_ref[...]
    mix = _rms(y, g[1:2]).reshape(streams, rows, D_MODEL)
    o_ref[...] = x + gt1 * mix


def _pool_layer(x, halo_src, mod, g_norm, w_grp, scale, *, streams, rows, hist,
                halo_is_h):
    bsz, seq, _ = x.shape
    grid = (bsz // streams, seq // rows)
    tile = pl.BlockSpec((streams, rows, D_MODEL), lambda b, t: (b, t, 0))
    full = lambda a: pl.BlockSpec(a.shape, lambda b, t: (0,) * a.ndim)
    halo_blocks = rows // POOL_HALO
    if halo_is_h:
        halo_spec = pl.BlockSpec((streams, POOL_HALO, D_MODEL), lambda b, t: (b, 0, 0))
    else:
        halo_spec = pl.BlockSpec(
            (streams, POOL_HALO, D_MODEL),
            lambda b, t: (b, jnp.maximum(t * halo_blocks - 1, 0), 0))
    return pl.pallas_call(
        functools.partial(_pool_kernel, streams=streams, rows=rows, hist=hist,
                          halo_is_h=halo_is_h),
        grid=grid,
        in_specs=[
            tile, halo_spec,
            pl.BlockSpec((streams, N_MOD, D_MODEL), lambda b, t: (b, 0, 0)),
            full(g_norm), full(w_grp), full(scale),
        ],
        out_specs=[
            tile,
            pl.BlockSpec((streams, POOL_HALO, D_MODEL), lambda b, t: (b, 0, 0)),
        ],
        out_shape=[
            jax.ShapeDtypeStruct(x.shape, _F32),
            jax.ShapeDtypeStruct((bsz, POOL_HALO, D_MODEL), _F32),
        ],
        scratch_shapes=[pltpu.VMEM((streams, rows + POOL_HALO, D_MODEL), _F32)],
        compiler_params=pltpu.CompilerParams(
            dimension_semantics=("arbitrary", "arbitrary"),
            vmem_limit_bytes=VMEM_LIMIT_BYTES),
        name="pool_mixer",
    )(x, halo_src, mod, g_norm, w_grp, scale)


def _ffn_kernel(x_ref, mod_ref, g_ref, w_up_ref, w_down_ref, o_ref, *, streams, rows):
    m = streams * rows
    x = x_ref[...]
    mod = mod_ref[...]
    sh2, sc2, gt2 = mod[:, 3:4], mod[:, 4:5], mod[:, 5:6]
    g = g_ref[...]
    h = _rms(x, g[2:3]) * (1.0 + sc2) + sh2
    hb = h.reshape(m, D_MODEL).astype(_BF16)
    acc = jnp.zeros((m, D_MODEL), _F32)
    for c in range(D_FF // FF_CHUNK):
        cols = slice(c * FF_CHUNK, (c + 1) * FF_CHUNK)
        a = jnp.maximum(_dot(hb, w_up_ref[:, cols]), 0.0)
        acc = acc + _dot((a * a).astype(_BF16), w_down_ref[cols, :])
    f = _rms(acc, g[3:4]).reshape(streams, rows, D_MODEL)
    o_ref[...] = x + gt2 * f


def _ffn_layer(x, mod, g_norm, w_up, w_down, *, streams, rows):
    bsz, seq, _ = x.shape
    grid = (bsz // streams, seq // rows)
    tile = pl.BlockSpec((streams, rows, D_MODEL), lambda b, t: (b, t, 0))
    full = lambda a: pl.BlockSpec(a.shape, lambda b, t: (0,) * a.ndim)
    return pl.pallas_call(
        functools.partial(_ffn_kernel, streams=streams, rows=rows),
        grid=grid,
        in_specs=[
            tile,
            pl.BlockSpec((streams, N_MOD, D_MODEL), lambda b, t: (b, 0, 0)),
            full(g_norm), full(w_up), full(w_down),
        ],
        out_specs=tile,
        out_shape=jax.ShapeDtypeStruct(x.shape, _F32),
        compiler_params=pltpu.CompilerParams(
            dimension_semantics=("arbitrary", "arbitrary"),
            vmem_limit_bytes=VMEM_LIMIT_BYTES),
        name="channel_mlp",
    )(x, mod, g_norm, w_up, w_down)


def _sgu_mask(length):
    pos = jnp.arange(SGU_LEN)
    same_chunk = (pos[:, None] // length) == (pos[None, :] // length)
    causal = (pos[None, :] % length) // CHUNK <= (pos[:, None] % length) // CHUNK
    return (same_chunk & causal).astype(_F32)


def _trunk(x, mod, pool_cache, weights, *, streams, rows):
    (g_norm, sgu_w_in, sgu_g_v, sgu_w_s, sgu_b_s, sgu_w_out, pool_w_grp, pool_scale,
     ffn_w_up, ffn_w_down) = weights
    sample = pool_cache is not None
    seq = x.shape[1]
    length = seq if sample else SGU_LEN
    reps = SGU_LEN // length
    mask = _sgu_mask(length)
    sgu_states, pool_states = [], []
    for i in range(DEPTH):
        j = i // 2
        if i % 2 == 0:
            w_s = jnp.tile(sgu_w_s[j][:, :length, :length], (1, reps, reps))
            b_s = jnp.tile(sgu_b_s[j][:, :length], (1, reps))[:, :, None]
            outs = _sgu_layer(x, mod[i], g_norm[i], sgu_w_in[j], sgu_g_v[j][None, :],
                              w_s, mask, b_s, sgu_w_out[j], streams=streams, rows=rows,
                              emit_v=sample)
            x = outs[0]
            if sample:
                sgu_states.append(outs[1])
        else:
            if sample:
                halo = jnp.pad(pool_cache[j], ((0, 0), (1, 0), (0, 0)))
            else:
                halo = x
            x, tail = _pool_layer(x, halo, mod[i], g_norm[i], pool_w_grp[j],
                                  pool_scale[j][None, :], streams=streams, rows=rows,
                                  hist=POOL_STATE if sample else 0, halo_is_h=sample)
            pool_states.append(tail[:, 1:, :])
        x = _ffn_layer(x, mod[i], g_norm[i], ffn_w_up[i], ffn_w_down[i],
                       streams=streams, rows=rows)
    return x, sgu_states, pool_states


def kernel(x_prompt, x_sample, cache_pool, c_prompt, c_sample, g_norm, w_ada, b_ada,
           sgu_w_in, sgu_g_v, sgu_w_s, sgu_b_s, sgu_w_out, pool_w_grp, pool_scale,
           ffn_w_up, ffn_w_down):
    n_p, n_s = c_prompt.shape[0], c_sample.shape[0]
    c_all = jnp.concatenate(
        [c_prompt, c_sample, jnp.zeros((MOD_ROWS - n_p - n_s, D_MODEL), _F32)], axis=0)
    mod = _modulation(c_all, w_ada, b_ada)
    mod = mod.reshape(DEPTH, MOD_ROWS, N_MOD, D_MODEL)
    mod_p, mod_s = mod[:, :n_p], mod[:, n_p:n_p + n_s]

    weights = (g_norm, sgu_w_in.astype(_BF16), sgu_g_v, sgu_w_s, sgu_b_s,
               sgu_w_out.astype(_BF16), pool_w_grp.astype(_BF16), pool_scale,
               ffn_w_up.astype(_BF16), ffn_w_down.astype(_BF16))

    y_p, _, pool_p = _trunk(x_prompt, mod_p, None, weights,
                            streams=1, rows=TILE_TOKENS)
    s_rows = x_sample.shape[1]
    y_s, sgu_s, pool_s = _trunk(x_sample, mod_s, cache_pool, weights,
                                streams=TILE_TOKENS // s_rows, rows=s_rows)
    return (y_p, y_s, jnp.stack(pool_p, axis=0), jnp.stack(sgu_s, axis=0),
            jnp.stack(pool_s, axis=0))
```

```python
import functools

import jax
import jax.numpy as jnp
from jax import lax
from jax.experimental import pallas as pl
from jax.experimental.pallas import tpu as pltpu

D_MODEL = 1024
DEPTH = 4
N_MOD = 6
EPS = 1e-6
D_SGU = 2 * D_MODEL
SGU_GROUPS = 8
SGU_GROUP_DIM = D_SGU // SGU_GROUPS
SGU_LEN = 128
SGU_PAIR = 2
CHUNK = 64
POOL_WINDOWS = (2, 4, 8, 16)
POOL_GROUP_DIM = D_MODEL // len(POOL_WINDOWS)
POOL_HALO = max(POOL_WINDOWS)
POOL_STATE = POOL_HALO - 1
D_FF = 4 * D_MODEL

TILE_TOKENS = 512
FF_CHUNK = 512
N_SLICES = D_FF // FF_CHUNK
MOD_COLS = 1536
MOD_ROWS = 32
VMEM_LIMIT_BYTES = 56 * 1024 * 1024

_F32 = jnp.float32
_BF16 = jnp.bfloat16


def _dot(a, b):
    return jnp.dot(a, b, preferred_element_type=_F32)


def _rms(x, g):
    return x * lax.rsqrt(jnp.mean(x * x, axis=-1, keepdims=True) + EPS) * g


def _gelu_tanh(x):
    c = 0.7978845608028654
    half = 0.5 * x
    return half + half * jnp.tanh(x * (c + (c * 0.044715) * (x * x)))


def _mod_kernel(c_ref, w_ref, b_ref, o_ref):
    c = c_ref[...]
    s = (c * jax.nn.sigmoid(c)).astype(_BF16)
    o_ref[0] = _dot(s, w_ref[0].astype(_BF16)) + b_ref[0]


def _modulation(c_all, w_ada, b_ada):
    n_col = N_MOD * D_MODEL
    return pl.pallas_call(
        _mod_kernel,
        grid=(DEPTH, n_col // MOD_COLS),
        in_specs=[
            pl.BlockSpec((MOD_ROWS, D_MODEL), lambda i, j: (0, 0)),
            pl.BlockSpec((1, D_MODEL, MOD_COLS), lambda i, j: (i, 0, j)),
            pl.BlockSpec((1, 1, MOD_COLS), lambda i, j: (i, 0, j)),
        ],
        out_specs=pl.BlockSpec((1, MOD_ROWS, MOD_COLS), lambda i, j: (i, 0, j)),
        out_shape=jax.ShapeDtypeStruct((DEPTH, MOD_ROWS, n_col), _F32),
        compiler_params=pltpu.CompilerParams(
            dimension_semantics=("arbitrary", "arbitrary"),
            vmem_limit_bytes=VMEM_LIMIT_BYTES),
        name="adaln_mod",
    )(c_all, w_ada, b_ada.reshape(DEPTH, 1, n_col))


def _sgu_kernel(x_ref, mod_ref, g_ref, w_in_ref, gv_ref, ws_ref, mask_ref, bs_ref,
                w_out_ref, o_ref, *v_ref, streams, rows):
    m = streams * rows
    x = x_ref[...]
    mod = mod_ref[...]
    sh1, sc1, gt1 = mod[:, 0:1], mod[:, 1:2], mod[:, 2:3]
    g = g_ref[...]
    h = _rms(x, g[0:1]) * (1.0 + sc1) + sh1
    hb = h.reshape(m, D_MODEL).astype(_BF16)

    v = _gelu_tanh(_dot(hb, w_in_ref[:, D_SGU:]))
    v = _rms(v, gv_ref[...])
    if v_ref:
        v_ref[0][...] = v.reshape(streams, rows, D_SGU)
    vb = v.astype(_BF16)

    mask = mask_ref[...]
    n_chunks = m // SGU_LEN
    acc = jnp.zeros((m, D_MODEL), _F32)
    for pair in range(SGU_GROUPS // SGU_PAIR):
        pcols = slice(pair * SGU_PAIR * SGU_GROUP_DIM, (pair + 1) * SGU_PAIR * SGU_GROUP_DIM)
        u = _gelu_tanh(_dot(hb, w_in_ref[:, pcols]))
        s_pair = []
        for grp in range(pair * SGU_PAIR, (pair + 1) * SGU_PAIR):
            cols = slice(grp * SGU_GROUP_DIM, (grp + 1) * SGU_GROUP_DIM)
            w = (ws_ref[grp] * mask).astype(_BF16)
            v_wide = jnp.concatenate(
                [vb[n * SGU_LEN:(n + 1) * SGU_LEN, cols] for n in range(n_chunks)], axis=1)
            s_wide = _dot(w, v_wide) + bs_ref[grp]
            s_pair.append(jnp.concatenate(
                [s_wide[:, n * SGU_GROUP_DIM:(n + 1) * SGU_GROUP_DIM]
                 for n in range(n_chunks)], axis=0))
        s = jnp.concatenate(s_pair, axis=1)
        acc = acc + _dot((u * s).astype(_BF16), w_out_ref[pcols, :])

    mix = _rms(acc, g[1:2]).reshape(streams, rows, D_MODEL)
    o_ref[...] = x + gt1 * mix


def _sgu_layer(x, mod, g_norm, w_in, g_v, w_s, mask, b_s, w_out, *, streams, rows,
               emit_v):
    bsz, seq, _ = x.shape
    grid = (bsz // streams, seq // rows)
    tile = lambda d: pl.BlockSpec((streams, rows, d), lambda b, t: (b, t, 0))
    full = lambda a: pl.BlockSpec(a.shape, lambda b, t: (0,) * a.ndim)
    out_shape = [jax.ShapeDtypeStruct(x.shape, _F32)]
    out_specs = [tile(D_MODEL)]
    if emit_v:
        out_shape.append(jax.ShapeDtypeStruct((bsz, seq, D_SGU), _F32))
        out_specs.append(tile(D_SGU))
    return pl.pallas_call(
        functools.partial(_sgu_kernel, streams=streams, rows=rows),
        grid=grid,
        in_specs=[
            tile(D_MODEL),
            pl.BlockSpec((streams, N_MOD, D_MODEL), lambda b, t: (b, 0, 0)),
            full(g_norm), full(w_in), full(g_v), full(w_s), full(mask), full(b_s),
            full(w_out),
        ],
        out_specs=out_specs,
        out_shape=out_shape,
        compiler_params=pltpu.CompilerParams(
            dimension_semantics=("arbitrary", "arbitrary"),
            vmem_limit_bytes=VMEM_LIMIT_BYTES),
        name="sgu_mixer",
    )(x, mod, g_norm, w_in, g_v, w_s, mask, b_s, w_out)


def _ffn_kernel(x_ref, mod_ref, g_ref, w_up_ref, w_down_ref, o_ref, *, streams, rows):
    m = streams * rows
    x = x_ref[...]
    mod = mod_ref[...]
    sh2, sc2, gt2 = mod[:, 3:4], mod[:, 4:5], mod[:, 5:6]
    g = g_ref[...]
    h = _rms(x, g[2:3]) * (1.0 + sc2) + sh2
    hb = h.reshape(m, D_MODEL).astype(_BF16)
    acc = jnp.zeros((m, D_MODEL), _F32)
    for c in range(D_FF // FF_CHUNK):
        cols = slice(c * FF_CHUNK, (c + 1) * FF_CHUNK)
        a = jnp.maximum(_dot(hb, w_up_ref[:, cols]), 0.0)
        acc = acc + _dot((a * a).astype(_BF16), w_down_ref[cols, :])
    f = _rms(acc, g[3:4]).reshape(streams, rows, D_MODEL)
    o_ref[...] = x + gt2 * f


def _ffn_layer(x, mod, g_norm, w_up, w_down, *, streams, rows):
    bsz, seq, _ = x.shape
    grid = (bsz // streams, seq // rows)
    tile = pl.BlockSpec((streams, rows, D_MODEL), lambda b, t: (b, t, 0))
    full = lambda a: pl.BlockSpec(a.shape, lambda b, t: (0,) * a.ndim)
    return pl.pallas_call(
        functools.partial(_ffn_kernel, streams=streams, rows=rows),
        grid=grid,
        in_specs=[
            tile,
            pl.BlockSpec((streams, N_MOD, D_MODEL), lambda b, t: (b, 0, 0)),
            full(g_norm), full(w_up), full(w_down),
        ],
        out_specs=tile,
        out_shape=jax.ShapeDtypeStruct(x.shape, _F32),
        compiler_params=pltpu.CompilerParams(
            dimension_semantics=("arbitrary", "arbitrary"),
            vmem_limit_bytes=VMEM_LIMIT_BYTES),
        name="channel_mlp",
    )(x, mod, g_norm, w_up, w_down)


def _window_sums(ext, win):
    tot, span = ext, 1
    while span < win:
        tot = tot + pltpu.roll(tot, span, axis=1)
        span *= 2
    return tot


def _zero_after(*vals):
    tot = None
    for v in vals:
        r = v.reshape(v.shape[0] // 8, 8, D_MODEL).sum(axis=0)
        for c in range(D_MODEL // 128):
            piece = r[:, c * 128:(c + 1) * 128]
            tot = piece if tot is None else tot + piece
    bits = pltpu.bitcast(tot, jnp.uint32)
    return pltpu.bitcast((bits >> 16) >> 16, _F32)


def _pool_mlp_kernel(x_ref, halo_ref, modp_ref, modq_ref, g_ref, wg_ref, scale_ref,
                     w_up_ref, w_down_ref, o_ref, tail_ref, hb_cur, hb_next, acc_cur,
                     acc_prev, x1_ring, carry_ref, *, streams, rows, tiles_per_stream,
                     n_tiles, hist, halo_is_h):
    by_rows = streams == 1
    sl_streams = streams if by_rows else streams // N_SLICES
    sl_rows = rows // N_SLICES if by_rows else rows
    sl_m = sl_streams * sl_rows
    s = pl.program_id(0)
    tile_pos = jnp.minimum(s, n_tiles - 1) % tiles_per_stream
    g = g_ref[...]

    def part(k):
        if by_rows:
            return (slice(None), slice(k * sl_rows, (k + 1) * sl_rows))
        return (slice(k * sl_streams, (k + 1) * sl_streams), slice(None))

    def part_streams(k):
        return slice(None) if by_rows else slice(k * sl_streams, (k + 1) * sl_streams)

    def flat(k):
        return slice(k * sl_m, (k + 1) * sl_m)

    @pl.when(s == 0)
    def _():
        hb_next[...] = jnp.zeros_like(hb_next)
        acc_cur[...] = jnp.zeros_like(acc_cur)
        x1_ring[...] = jnp.zeros_like(x1_ring)
        carry_ref[...] = jnp.zeros_like(carry_ref)

    def main(k):
        cols = slice(k * FF_CHUNK, (k + 1) * FF_CHUNK)
        a = jnp.maximum(_dot(hb_cur[...], w_up_ref[:, cols]), 0.0)
        d = _dot((a * a).astype(_BF16), w_down_ref[cols, :])
        acc_cur[...] = d if k == 0 else acc_cur[...] + d

    def pre(k):
        x = x_ref[part(k)]
        mod = modp_ref[part_streams(k)]
        sh1, sc1, gt1 = mod[:, 0:1], mod[:, 1:2], mod[:, 2:3]
        h = _rms(x, g[0:1]) * (1.0 + sc1) + sh1
        if halo_is_h:
            halo = halo_ref[part_streams(k)]
        else:
            halo = carry_ref[...]
            if k == 0:
                halo = jnp.where(tile_pos > 0, halo, 0.0)
        ext = jnp.concatenate([halo, h], axis=1)
        last = ext[:, sl_rows:, :]
        if halo_is_h:
            tail_ref[part_streams(k)] = last
        else:
            carry_ref[...] = last
            if k == N_SLICES - 1:
                tail_ref[...] = last
        base = tile_pos * rows + k * sl_rows if by_rows else 0
        pos = lax.broadcasted_iota(jnp.int32, (1, sl_rows, 1), 1) + (base + hist)
        outs = []
        for grp, win in enumerate(POOL_WINDOWS):
            cols = slice(grp * POOL_GROUP_DIM, (grp + 1) * POOL_GROUP_DIM)
            tot = _window_sums(ext[:, :, cols], win)[:, POOL_HALO:, :]
            cnt = jnp.minimum(win, pos + 1).astype(_F32)
            pooled = (tot / cnt - h[:, :, cols]).reshape(sl_m, POOL_GROUP_DIM)
            outs.append(_dot(pooled.astype(_BF16), wg_ref[grp]))
        y = jnp.concatenate(outs, axis=-1) * scale_ref[...]
        x = x + gt1 * _rms(y, g[1:2]).reshape(sl_streams, sl_rows, D_MODEL)
        x1_ring[s % 3, flat(k)] = x.reshape(sl_m, D_MODEL)
        sh2, sc2 = mod[:, 3:4], mod[:, 4:5]
        h2 = (_rms(x, g[2:3]) * (1.0 + sc2) + sh2).reshape(sl_m, D_MODEL)
        hb_next[flat(k)] = h2.astype(_BF16)
        return h2

    def post(k):
        gt2 = modq_ref[part_streams(k)][:, 5:6]
        x = x1_ring[(s + 1) % 3, flat(k)].reshape(sl_streams, sl_rows, D_MODEL)
        f = _rms(acc_prev[flat(k)], g[3:4]).reshape(sl_streams, sl_rows, D_MODEL)
        out = x + gt2 * f
        o_ref[part(k)] = out
        return out.reshape(sl_m, D_MODEL)

    hb_cur[...] = hb_next[...]
    acc_prev[...] = acc_cur[...]
    for k in range(N_SLICES):
        main(k)
        done = _zero_after(pre(k), post(k))
        if k < N_SLICES - 1:
            acc_cur[0:8, 0:128] = acc_cur[0:8, 0:128] + done


def _pool_mlp_layer(x, halo, mod, g_norm, w_grp, scale, w_up, w_down, *, streams, rows,
                    hist, halo_is_h):
    bsz, seq, _ = x.shape
    tiles_per_stream = seq // rows
    n_tiles = (bsz // streams) * tiles_per_stream
    m = streams * rows

    def cur(s):
        return jnp.minimum(s, n_tiles - 1)

    def done(s):
        return jnp.clip(s - 2, 0, n_tiles - 1)

    def at(which):
        return pl.BlockSpec(
            (streams, rows, D_MODEL),
            lambda s: (which(s) // tiles_per_stream, which(s) % tiles_per_stream, 0))

    def per_stream(which, n):
        return pl.BlockSpec((streams, n, D_MODEL),
                            lambda s: (which(s) // tiles_per_stream, 0, 0))

    def full(a):
        return pl.BlockSpec(a.shape, lambda s: (0,) * a.ndim,
                            pipeline_mode=pl.Buffered(1))

    return pl.pallas_call(
        functools.partial(_pool_mlp_kernel, streams=streams, rows=rows,
                          tiles_per_stream=tiles_per_stream, n_tiles=n_tiles, hist=hist,
                          halo_is_h=halo_is_h),
        grid=(n_tiles + 2,),
        in_specs=[at(cur), per_stream(cur, POOL_HALO), per_stream(cur, N_MOD),
                  per_stream(done, N_MOD), full(g_norm), full(w_grp), full(scale),
                  full(w_up), full(w_down)],
        out_specs=[at(done), per_stream(cur, POOL_HALO)],
        out_shape=[jax.ShapeDtypeStruct(x.shape, _F32),
                   jax.ShapeDtypeStruct((bsz, POOL_HALO, D_MODEL), _F32)],
        scratch_shapes=[pltpu.VMEM((m, D_MODEL), _BF16), pltpu.VMEM((m, D_MODEL), _BF16),
                        pltpu.VMEM((m, D_MODEL), _F32), pltpu.VMEM((m, D_MODEL), _F32),
                        pltpu.VMEM((3, m, D_MODEL), _F32),
                        pltpu.VMEM((streams, POOL_HALO, D_MODEL), _F32)],
        compiler_params=pltpu.CompilerParams(
            dimension_semantics=("arbitrary",),
            vmem_limit_bytes=VMEM_LIMIT_BYTES),
        name="pool_mlp_block",
    )(x, halo, mod, mod, g_norm, w_grp, scale, w_up, w_down)


def _sgu_mask(length):
    pos = jnp.arange(SGU_LEN)
    same_chunk = (pos[:, None] // length) == (pos[None, :] // length)
    causal = (pos[None, :] % length) // CHUNK <= (pos[:, None] % length) // CHUNK
    return (same_chunk & causal).astype(_F32)


def _trunk(x, mod, pool_cache, weights, *, streams, rows):
    (g_norm, sgu_w_in, sgu_g_v, sgu_w_s, sgu_b_s, sgu_w_out, pool_w_grp, pool_scale,
     ffn_w_up, ffn_w_down) = weights
    sample = pool_cache is not None
    seq = x.shape[1]
    length = seq if sample else SGU_LEN
    reps = SGU_LEN // length
    mask = _sgu_mask(length)
    sgu_states, pool_states = [], []
    for i in range(DEPTH):
        j = i // 2
        if i % 2 == 0:
            w_s = jnp.tile(sgu_w_s[j][:, :length, :length], (1, reps, reps))
            b_s = jnp.tile(sgu_b_s[j][:, :length], (1, reps))[:, :, None]
            outs = _sgu_layer(x, mod[i], g_norm[i], sgu_w_in[j], sgu_g_v[j][None, :],
                              w_s, mask, b_s, sgu_w_out[j], streams=streams, rows=rows,
                              emit_v=sample)
            x = outs[0]
            if sample:
                sgu_states.append(outs[1])
            x = _ffn_layer(x, mod[i], g_norm[i], ffn_w_up[i], ffn_w_down[i],
                           streams=streams, rows=rows)
        else:
            if sample:
                halo = jnp.pad(pool_cache[j], ((0, 0), (1, 0), (0, 0)))
            else:
                halo = jnp.zeros((x.shape[0], POOL_HALO, D_MODEL), _F32)
            x, tail = _pool_mlp_layer(
                x, halo, mod[i], g_norm[i], pool_w_grp[j], pool_scale[j][None, :],
                ffn_w_up[i], ffn_w_down[i], streams=streams, rows=rows,
                hist=POOL_STATE if sample else 0, halo_is_h=sample)
            pool_states.append(tail[:, 1:, :])
    return x, sgu_states, pool_states


def kernel(x_prompt, x_sample, cache_pool, c_prompt, c_sample, g_norm, w_ada, b_ada,
           sgu_w_in, sgu_g_v, sgu_w_s, sgu_b_s, sgu_w_out, pool_w_grp, pool_scale,
           ffn_w_up, ffn_w_down):
    n_p, n_s = c_prompt.shape[0], c_sample.shape[0]
    c_all = jnp.concatenate(
        [c_prompt, c_sample, jnp.zeros((MOD_ROWS - n_p - n_s, D_MODEL), _F32)], axis=0)
    mod = _modulation(c_all, w_ada, b_ada)
    mod = mod.reshape(DEPTH, MOD_ROWS, N_MOD, D_MODEL)
    mod_p, mod_s = mod[:, :n_p], mod[:, n_p:n_p + n_s]

    weights = (g_norm, sgu_w_in.astype(_BF16), sgu_g_v, sgu_w_s, sgu_b_s,
               sgu_w_out.astype(_BF16), pool_w_grp.astype(_BF16), pool_scale,
               ffn_w_up.astype(_BF16), ffn_w_down.astype(_BF16))

    y_p, _, pool_p = _trunk(x_prompt, mod_p, None, weights,
                            streams=1, rows=TILE_TOKENS)
    s_rows = x_sample.shape[1]
    y_s, sgu_s, pool_s = _trunk(x_sample, mod_s, cache_pool, weights,
                                streams=TILE_TOKENS // s_rows, rows=s_rows)
    return (y_p, y_s, jnp.stack(pool_p, axis=0), jnp.stack(sgu_s, axis=0),
            jnp.stack(pool_s, axis=0))
```

```python
import functools

import jax
import jax.numpy as jnp
from jax import lax
from jax.experimental import pallas as pl
from jax.experimental.pallas import tpu as pltpu

D_MODEL = 1024
DEPTH = 4
N_MOD = 6
EPS = 1e-6
D_SGU = 2 * D_MODEL
SGU_GROUPS = 8
SGU_GROUP_DIM = D_SGU // SGU_GROUPS
SGU_LEN = 128
SGU_PAIR = 2
CHUNK = 64
POOL_WINDOWS = (2, 4, 8, 16)
POOL_GROUP_DIM = D_MODEL // len(POOL_WINDOWS)
POOL_HALO = max(POOL_WINDOWS)
POOL_STATE = POOL_HALO - 1
D_FF = 4 * D_MODEL

TILE_TOKENS = 512
FF_CHUNK = 512
N_SLICES = D_FF // FF_CHUNK
MIN_PIPELINED_TILES = 16
MOD_COLS = 1536
MOD_ROWS = 32
VMEM_LIMIT_BYTES = 56 * 1024 * 1024

_F32 = jnp.float32
_BF16 = jnp.bfloat16


def _dot(a, b):
    return jnp.dot(a, b, preferred_element_type=_F32)


def _rms(x, g):
    return x * lax.rsqrt(jnp.mean(x * x, axis=-1, keepdims=True) + EPS) * g


def _layer_spec(stacked, layer, **kwargs):
    index = (layer,) + (0,) * (stacked.ndim - 1)
    return pl.BlockSpec((None,) + stacked.shape[1:], lambda *_: index, **kwargs)


def _gelu_tanh(x):
    c = 0.7978845608028654
    half = 0.5 * x
    return half + half * jnp.tanh(x * (c + (c * 0.044715) * (x * x)))


def _mod_kernel(c_ref, w_ref, b_ref, o_ref):
    c = c_ref[...]
    s = (c * jax.nn.sigmoid(c)).astype(_BF16)
    o_ref[0] = _dot(s, w_ref[0].astype(_BF16)) + b_ref[0]


def _modulation(c_all, w_ada, b_ada):
    n_col = N_MOD * D_MODEL
    return pl.pallas_call(
        _mod_kernel,
        grid=(DEPTH, n_col // MOD_COLS),
        in_specs=[
            pl.BlockSpec((MOD_ROWS, D_MODEL), lambda i, j: (0, 0)),
            pl.BlockSpec((1, D_MODEL, MOD_COLS), lambda i, j: (i, 0, j)),
            pl.BlockSpec((1, 1, MOD_COLS), lambda i, j: (i, 0, j)),
        ],
        out_specs=pl.BlockSpec((1, MOD_ROWS, MOD_COLS), lambda i, j: (i, 0, j)),
        out_shape=jax.ShapeDtypeStruct((DEPTH, MOD_ROWS, n_col), _F32),
        compiler_params=pltpu.CompilerParams(
            dimension_semantics=("arbitrary", "arbitrary"),
            vmem_limit_bytes=VMEM_LIMIT_BYTES),
        name="adaln_mod",
    )(c_all, w_ada, b_ada.reshape(DEPTH, 1, n_col))


def _sgu_kernel(x_ref, mod_ref, g_ref, w_in_ref, gv_ref, ws_ref, mask_ref, bs_ref,
                w_out_ref, o_ref, *v_ref, streams, rows):
    m = streams * rows
    x = x_ref[...]
    mod = mod_ref[...]
    sh1, sc1, gt1 = mod[:, 0:1], mod[:, 1:2], mod[:, 2:3]
    g = g_ref[...]
    h = _rms(x, g[0:1]) * (1.0 + sc1) + sh1
    hb = h.reshape(m, D_MODEL).astype(_BF16)

    n_pairs = SGU_GROUPS // SGU_PAIR
    pair_cols = [slice(p * SGU_PAIR * SGU_GROUP_DIM, (p + 1) * SGU_PAIR * SGU_GROUP_DIM)
                 for p in range(n_pairs)]
    v = _rms(_gelu_tanh(_dot(hb, w_in_ref[:, D_SGU:])), gv_ref[...])
    if v_ref:
        v_ref[0][...] = v.reshape(streams, rows, D_SGU)
    vb = v.astype(_BF16)

    mask = mask_ref[...]
    n_chunks = m // SGU_LEN
    acc = jnp.zeros((m, D_MODEL), _F32)
    u_next = _gelu_tanh(_dot(hb, w_in_ref[:, pair_cols[0]]))
    for pair in range(n_pairs):
        pcols = pair_cols[pair]
        u = u_next
        if pair + 1 < n_pairs:
            u_next = _gelu_tanh(_dot(hb, w_in_ref[:, pair_cols[pair + 1]]))
        s_pair = []
        for grp in range(pair * SGU_PAIR, (pair + 1) * SGU_PAIR):
            cols = slice(grp * SGU_GROUP_DIM, (grp + 1) * SGU_GROUP_DIM)
            w = (ws_ref[grp] * mask).astype(_BF16)
            v_wide = jnp.concatenate(
                [vb[n * SGU_LEN:(n + 1) * SGU_LEN, cols] for n in range(n_chunks)], axis=1)
            s_wide = _dot(w, v_wide) + bs_ref[grp]
            s_pair.append(jnp.concatenate(
                [s_wide[:, n * SGU_GROUP_DIM:(n + 1) * SGU_GROUP_DIM]
                 for n in range(n_chunks)], axis=0))
        s = jnp.concatenate(s_pair, axis=1)
        acc = acc + _dot((u * s).astype(_BF16), w_out_ref[pcols, :])

    mix = _rms(acc, g[1:2]).reshape(streams, rows, D_MODEL)
    o_ref[...] = x + gt1 * mix


def _sgu_layer(x, mod, g_norm, w_in, g_v, w_s, mask, b_s, w_out, *, layer, streams, rows,
               emit_v):
    bsz, seq, _ = x.shape
    grid = (bsz // streams, seq // rows)
    tile = lambda d: pl.BlockSpec((streams, rows, d), lambda b, t: (b, t, 0))
    full = lambda a: pl.BlockSpec(a.shape, lambda b, t: (0,) * a.ndim)
    out_shape = [jax.ShapeDtypeStruct(x.shape, _F32)]
    out_specs = [tile(D_MODEL)]
    if emit_v:
        out_shape.append(jax.ShapeDtypeStruct((bsz, seq, D_SGU), _F32))
        out_specs.append(tile(D_SGU))
    return pl.pallas_call(
        functools.partial(_sgu_kernel, streams=streams, rows=rows),
        grid=grid,
        in_specs=[
            tile(D_MODEL),
            pl.BlockSpec((streams, N_MOD, D_MODEL), lambda b, t: (b, 0, 0)),
            full(g_norm), _layer_spec(w_in, layer), full(g_v), full(w_s), full(mask),
            full(b_s), _layer_spec(w_out, layer),
        ],
        out_specs=out_specs,
        out_shape=out_shape,
        compiler_params=pltpu.CompilerParams(
            dimension_semantics=("arbitrary", "arbitrary"),
            vmem_limit_bytes=VMEM_LIMIT_BYTES),
        name="sgu_mixer",
    )(x, mod, g_norm, w_in, g_v, w_s, mask, b_s, w_out)


def _ffn_kernel(x_ref, mod_ref, g_ref, w_up_ref, w_down_ref, o_ref, *, streams, rows):
    m = streams * rows
    x = x_ref[...]
    mod = mod_ref[...]
    sh2, sc2, gt2 = mod[:, 3:4], mod[:, 4:5], mod[:, 5:6]
    g = g_ref[...]
    h = _rms(x, g[2:3]) * (1.0 + sc2) + sh2
    hb = h.reshape(m, D_MODEL).astype(_BF16)
    acc = jnp.zeros((m, D_MODEL), _F32)
    for c in range(D_FF // FF_CHUNK):
        cols = slice(c * FF_CHUNK, (c + 1) * FF_CHUNK)
        a = jnp.maximum(_dot(hb, w_up_ref[:, cols]), 0.0)
        acc = acc + _dot((a * a).astype(_BF16), w_down_ref[cols, :])
    f = _rms(acc, g[3:4]).reshape(streams, rows, D_MODEL)
    o_ref[...] = x + gt2 * f


def _ffn_layer(x, mod, g_norm, w_up, w_down, *, layer, streams, rows):
    bsz, seq, _ = x.shape
    grid = (bsz // streams, seq // rows)
    tile = pl.BlockSpec((streams, rows, D_MODEL), lambda b, t: (b, t, 0))
    full = lambda a: pl.BlockSpec(a.shape, lambda b, t: (0,) * a.ndim)
    return pl.pallas_call(
        functools.partial(_ffn_kernel, streams=streams, rows=rows),
        grid=grid,
        in_specs=[
            tile,
            pl.BlockSpec((streams, N_MOD, D_MODEL), lambda b, t: (b, 0, 0)),
            full(g_norm), _layer_spec(w_up, layer), _layer_spec(w_down, layer),
        ],
        out_specs=tile,
        out_shape=jax.ShapeDtypeStruct(x.shape, _F32),
        compiler_params=pltpu.CompilerParams(
            dimension_semantics=("arbitrary", "arbitrary"),
            vmem_limit_bytes=VMEM_LIMIT_BYTES),
        name="channel_mlp",
    )(x, mod, g_norm, w_up, w_down)


def _window_sums(ext, win):
    tot, span = ext, 1
    while span < win:
        tot = tot + pltpu.roll(tot, span, axis=1)
        span *= 2
    return tot


def _zero_after(*vals):
    tot = None
    for v in vals:
        r = v.reshape(v.shape[0] // 8, 8, D_MODEL).sum(axis=0)
        for c in range(D_MODEL // 128):
            piece = r[:, c * 128:(c + 1) * 128]
            tot = piece if tot is None else tot + piece
    bits = pltpu.bitcast(tot, jnp.uint32)
    return pltpu.bitcast((bits >> 16) >> 16, _F32)


def _pool_mlp_kernel(x_ref, halo_ref, modp_ref, modq_ref, g_ref, wg_ref, scale_ref,
                     w_up_ref, w_down_ref, o_ref, tail_ref, hb_cur, hb_next, acc_cur,
                     acc_prev, x1_ring, carry_ref, *, streams, rows, tiles_per_stream,
                     n_tiles, hist, halo_is_h, skip_idle_main):
    by_rows = streams == 1
    sl_streams = streams if by_rows else streams // N_SLICES
    sl_rows = rows // N_SLICES if by_rows else rows
    sl_m = sl_streams * sl_rows
    s = pl.program_id(0)
    tile_pos = jnp.minimum(s, n_tiles - 1) % tiles_per_stream
    g = g_ref[...]

    def part(k):
        if by_rows:
            return (slice(None), slice(k * sl_rows, (k + 1) * sl_rows))
        return (slice(k * sl_streams, (k + 1) * sl_streams), slice(None))

    def part_streams(k):
        return slice(None) if by_rows else slice(k * sl_streams, (k + 1) * sl_streams)

    def flat(k):
        return slice(k * sl_m, (k + 1) * sl_m)

    @pl.when(s == 0)
    def _():
        hb_next[...] = jnp.zeros_like(hb_next)
        acc_cur[...] = jnp.zeros_like(acc_cur)
        x1_ring[...] = jnp.zeros_like(x1_ring)
        carry_ref[...] = jnp.zeros_like(carry_ref)

    def main(k):
        cols = slice(k * FF_CHUNK, (k + 1) * FF_CHUNK)
        a = jnp.maximum(_dot(hb_cur[...], w_up_ref[:, cols]), 0.0)
        d = _dot((a * a).astype(_BF16), w_down_ref[cols, :])
        acc_cur[...] = d if k == 0 else acc_cur[...] + d

    def pre(k):
        x = x_ref[part(k)]
        mod = modp_ref[part_streams(k)]
        sh1, sc1, gt1 = mod[:, 0:1], mod[:, 1:2], mod[:, 2:3]
        h = _rms(x, g[0:1]) * (1.0 + sc1) + sh1
        if halo_is_h:
            halo = halo_ref[part_streams(k)]
        else:
            halo = carry_ref[...]
            if k == 0:
                halo = jnp.where(tile_pos > 0, halo, 0.0)
        ext = jnp.concatenate([halo, h], axis=1)
        last = ext[:, sl_rows:, :]
        if halo_is_h:
            tail_ref[part_streams(k)] = last
        else:
            carry_ref[...] = last
            if k == N_SLICES - 1:
                tail_ref[...] = last
        base = tile_pos * rows + k * sl_rows if by_rows else 0
        pos = lax.broadcasted_iota(jnp.int32, (1, sl_rows, 1), 1) + (base + hist)
        outs = []
        for grp, win in enumerate(POOL_WINDOWS):
            cols = slice(grp * POOL_GROUP_DIM, (grp + 1) * POOL_GROUP_DIM)
            tot = _window_sums(ext[:, :, cols], win)[:, POOL_HALO:, :]
            inv_cnt = 1.0 / jnp.minimum(win, pos + 1).astype(_F32)
            pooled = (tot * inv_cnt - h[:, :, cols]).reshape(sl_m, POOL_GROUP_DIM)
            outs.append(_dot(pooled.astype(_BF16), wg_ref[grp]))
        y = jnp.concatenate(outs, axis=-1) * scale_ref[...]
        x = x + gt1 * _rms(y, g[1:2]).reshape(sl_streams, sl_rows, D_MODEL)
        x1_ring[s % 3, flat(k)] = x.reshape(sl_m, D_MODEL)
        sh2, sc2 = mod[:, 3:4], mod[:, 4:5]
        h2 = (_rms(x, g[2:3]) * (1.0 + sc2) + sh2).reshape(sl_m, D_MODEL)
        hb_next[flat(k)] = h2.astype(_BF16)
        return h2

    def post(k):
        gt2 = modq_ref[part_streams(k)][:, 5:6]
        x = x1_ring[(s + 1) % 3, flat(k)].reshape(sl_streams, sl_rows, D_MODEL)
        f = _rms(acc_prev[flat(k)], g[3:4]).reshape(sl_streams, sl_rows, D_MODEL)
        out = x + gt2 * f
        o_ref[part(k)] = out
        return out.reshape(sl_m, D_MODEL)

    hb_cur[...] = hb_next[...]
    acc_prev[...] = acc_cur[...]
    for k in range(N_SLICES):
        if skip_idle_main:
            pl.when(jnp.logical_and(s >= 1, s <= n_tiles))(functools.partial(main, k))
        else:
            main(k)
        done = _zero_after(pre(k), post(k))
        if k < N_SLICES - 1:
            acc_cur[0:8, 0:128] = acc_cur[0:8, 0:128] + done


def _pool_mlp_layer(x, halo, mod, g_norm, w_grp, scale, w_up, w_down, *, layer, streams,
                    rows, hist, halo_is_h):
    bsz, seq, _ = x.shape
    tiles_per_stream = seq // rows
    n_tiles = (bsz // streams) * tiles_per_stream
    m = streams * rows

    def cur(s):
        return jnp.minimum(s, n_tiles - 1)

    def done(s):
        return jnp.clip(s - 2, 0, n_tiles - 1)

    def at(which):
        return pl.BlockSpec(
            (streams, rows, D_MODEL),
            lambda s: (which(s) // tiles_per_stream, which(s) % tiles_per_stream, 0))

    def per_stream(which, n):
        return pl.BlockSpec((streams, n, D_MODEL),
                            lambda s: (which(s) // tiles_per_stream, 0, 0))

    def full(a):
        return pl.BlockSpec(a.shape, lambda s: (0,) * a.ndim,
                            pipeline_mode=pl.Buffered(1))

    return pl.pallas_call(
        functools.partial(_pool_mlp_kernel, streams=streams, rows=rows,
                          tiles_per_stream=tiles_per_stream, n_tiles=n_tiles, hist=hist,
                          halo_is_h=halo_is_h, skip_idle_main=n_tiles < MIN_PIPELINED_TILES),
        grid=(n_tiles + 2,),
        in_specs=[at(cur), per_stream(cur, POOL_HALO), per_stream(cur, N_MOD),
                  per_stream(done, N_MOD), full(g_norm), full(w_grp), full(scale),
                  _layer_spec(w_up, layer, pipeline_mode=pl.Buffered(1)),
                  _layer_spec(w_down, layer, pipeline_mode=pl.Buffered(1))],
        out_specs=[at(done), per_stream(cur, POOL_HALO)],
        out_shape=[jax.ShapeDtypeStruct(x.shape, _F32),
                   jax.ShapeDtypeStruct((bsz, POOL_HALO, D_MODEL), _F32)],
        scratch_shapes=[pltpu.VMEM((m, D_MODEL), _BF16), pltpu.VMEM((m, D_MODEL), _BF16),
                        pltpu.VMEM((m, D_MODEL), _F32), pltpu.VMEM((m, D_MODEL), _F32),
                        pltpu.VMEM((3, m, D_MODEL), _F32),
                        pltpu.VMEM((streams, POOL_HALO, D_MODEL), _F32)],
        compiler_params=pltpu.CompilerParams(
            dimension_semantics=("arbitrary",),
            vmem_limit_bytes=VMEM_LIMIT_BYTES),
        name="pool_mlp_block",
    )(x, halo, mod, mod, g_norm, w_grp, scale, w_up, w_down)


def _sgu_mask(length):
    pos = jnp.arange(SGU_LEN)
    same_chunk = (pos[:, None] // length) == (pos[None, :] // length)
    causal = (pos[None, :] % length) // CHUNK <= (pos[:, None] % length) // CHUNK
    return (same_chunk & causal).astype(_F32)


def _trunk(x, mod, pool_cache, weights, *, streams, rows):
    (g_norm, sgu_w_in, sgu_g_v, sgu_w_s, sgu_b_s, sgu_w_out, pool_w_grp, pool_scale,
     ffn_w_up, ffn_w_down) = weights
    sample = pool_cache is not None
    seq = x.shape[1]
    length = seq if sample else SGU_LEN
    reps = SGU_LEN // length
    mask = _sgu_mask(length)
    sgu_states, pool_states = [], []
    for i in range(DEPTH):
        j = i // 2
        if i % 2 == 0:
            w_s = jnp.tile(sgu_w_s[j][:, :length, :length], (1, reps, reps))
            b_s = jnp.tile(sgu_b_s[j][:, :length], (1, reps))[:, :, None]
            outs = _sgu_layer(x, mod[i], g_norm[i], sgu_w_in, sgu_g_v[j][None, :],
                              w_s, mask, b_s, sgu_w_out, layer=j, streams=streams,
                              rows=rows, emit_v=sample)
            x = outs[0]
            if sample:
                sgu_states.append(outs[1])
            x = _ffn_layer(x, mod[i], g_norm[i], ffn_w_up, ffn_w_down, layer=i,
                           streams=streams, rows=rows)
        else:
            if sample:
                halo = jnp.pad(pool_cache[j], ((0, 0), (1, 0), (0, 0)))
            else:
                halo = jnp.zeros((x.shape[0], POOL_HALO, D_MODEL), _F32)
            x, tail = _pool_mlp_layer(
                x, halo, mod[i], g_norm[i], pool_w_grp[j], pool_scale[j][None, :],
                ffn_w_up, ffn_w_down, layer=i, streams=streams, rows=rows,
                hist=POOL_STATE if sample else 0, halo_is_h=sample)
            pool_states.append(tail[:, 1:, :])
    return x, sgu_states, pool_states


def kernel(x_prompt, x_sample, cache_pool, c_prompt, c_sample, g_norm, w_ada, b_ada,
           sgu_w_in, sgu_g_v, sgu_w_s, sgu_b_s, sgu_w_out, pool_w_grp, pool_scale,
           ffn_w_up, ffn_w_down):
    n_p, n_s = c_prompt.shape[0], c_sample.shape[0]
    c_all = jnp.concatenate(
        [c_prompt, c_sample, jnp.zeros((MOD_ROWS - n_p - n_s, D_MODEL), _F32)], axis=0)
    mod = _modulation(c_all, w_ada, b_ada)
    mod = mod.reshape(DEPTH, MOD_ROWS, N_MOD, D_MODEL)
    mod_p, mod_s = mod[:, :n_p], mod[:, n_p:n_p + n_s]

    weights = (g_norm, sgu_w_in.astype(_BF16), sgu_g_v, sgu_w_s, sgu_b_s,
               sgu_w_out.astype(_BF16), pool_w_grp.astype(_BF16), pool_scale,
               ffn_w_up.astype(_BF16), ffn_w_down.astype(_BF16))

    y_p, _, pool_p = _trunk(x_prompt, mod_p, None, weights,
                            streams=1, rows=TILE_TOKENS)
    s_rows = x_sample.shape[1]
    y_s, sgu_s, pool_s = _trunk(x_sample, mod_s, cache_pool, weights,
                                streams=TILE_TOKENS // s_rows, rows=s_rows)
    return (y_p, y_s, jnp.stack(pool_p, axis=0), jnp.stack(sgu_s, axis=0),
            jnp.stack(pool_s, axis=0))
```

```python
import functools

import jax
import jax.numpy as jnp
from jax import lax
from jax.experimental import pallas as pl
from jax.experimental.pallas import tpu as pltpu

D_MODEL = 1024
DEPTH = 4
N_MOD = 6
EPS = 1e-6
D_SGU = 2 * D_MODEL
SGU_GROUPS = 8
SGU_GROUP_DIM = D_SGU // SGU_GROUPS
SGU_LEN = 128
SGU_PAIR = 2
CHUNK = 64
POOL_WINDOWS = (2, 4, 8, 16)
POOL_GROUP_DIM = D_MODEL // len(POOL_WINDOWS)
POOL_HALO = max(POOL_WINDOWS)
POOL_STATE = POOL_HALO - 1
D_FF = 4 * D_MODEL

TILE_TOKENS = 512
FF_CHUNK = 512
N_SLICES = D_FF // FF_CHUNK
MIN_PIPELINED_TILES = 16
MOD_COLS = 1536
MOD_ROWS = 32
VMEM_LIMIT_BYTES = 56 * 1024 * 1024

_F32 = jnp.float32
_BF16 = jnp.bfloat16


def _dot(a, b):
    return jnp.dot(a, b, preferred_element_type=_F32)


def _rms(x, g):
    return x * lax.rsqrt(jnp.mean(x * x, axis=-1, keepdims=True) + EPS) * g


def _layer_spec(stacked, layer, **kwargs):
    index = (layer,) + (0,) * (stacked.ndim - 1)
    return pl.BlockSpec((None,) + stacked.shape[1:], lambda *_: index, **kwargs)


def _gelu_tanh(x):
    c = 0.7978845608028654
    half = 0.5 * x
    return half + half * jnp.tanh(x * (c + (c * 0.044715) * (x * x)))


def _mod_kernel(c_ref, w_ref, b_ref, o_ref):
    c = c_ref[...]
    s = (c * jax.nn.sigmoid(c)).astype(_BF16)
    o_ref[0] = _dot(s, w_ref[0].astype(_BF16)) + b_ref[0]


def _modulation(c_all, w_ada, b_ada):
    n_col = N_MOD * D_MODEL
    return pl.pallas_call(
        _mod_kernel,
        grid=(DEPTH, n_col // MOD_COLS),
        in_specs=[
            pl.BlockSpec((MOD_ROWS, D_MODEL), lambda i, j: (0, 0)),
            pl.BlockSpec((1, D_MODEL, MOD_COLS), lambda i, j: (i, 0, j)),
            pl.BlockSpec((1, 1, MOD_COLS), lambda i, j: (i, 0, j)),
        ],
        out_specs=pl.BlockSpec((1, MOD_ROWS, MOD_COLS), lambda i, j: (i, 0, j)),
        out_shape=jax.ShapeDtypeStruct((DEPTH, MOD_ROWS, n_col), _F32),
        compiler_params=pltpu.CompilerParams(
            dimension_semantics=("arbitrary", "arbitrary"),
            vmem_limit_bytes=VMEM_LIMIT_BYTES),
        name="adaln_mod",
    )(c_all, w_ada, b_ada.reshape(DEPTH, 1, n_col))


def _sgu_kernel(x_ref, mod_ref, g_ref, w_in_ref, gv_ref, ws_ref, mask_ref, bs_ref,
                w_out_ref, o_ref, *v_ref, streams, rows):
    m = streams * rows
    x = x_ref[...]
    mod = mod_ref[...]
    sh1, sc1, gt1 = mod[:, 0:1], mod[:, 1:2], mod[:, 2:3]
    g = g_ref[...]
    h = _rms(x, g[0:1]) * (1.0 + sc1) + sh1
    hb = h.reshape(m, D_MODEL).astype(_BF16)

    n_pairs = SGU_GROUPS // SGU_PAIR
    pair_cols = [slice(p * SGU_PAIR * SGU_GROUP_DIM, (p + 1) * SGU_PAIR * SGU_GROUP_DIM)
                 for p in range(n_pairs)]
    v_parts, u_parts = [], []
    for cols in pair_cols:
        vcols = slice(D_SGU + cols.start, D_SGU + cols.stop)
        v_parts.append(_gelu_tanh(_dot(hb, w_in_ref[:, vcols])))
        u_parts.append(_gelu_tanh(_dot(hb, w_in_ref[:, cols])))
    v = _rms(jnp.concatenate(v_parts, axis=1), gv_ref[...])
    if v_ref:
        v_ref[0][...] = v.reshape(streams, rows, D_SGU)
    vb = v.astype(_BF16)

    mask = mask_ref[...]
    n_chunks = m // SGU_LEN
    acc = jnp.zeros((m, D_MODEL), _F32)
    for pair in range(n_pairs):
        pcols = pair_cols[pair]
        u = u_parts[pair]
        s_pair = []
        for grp in range(pair * SGU_PAIR, (pair + 1) * SGU_PAIR):
            cols = slice(grp * SGU_GROUP_DIM, (grp + 1) * SGU_GROUP_DIM)
            w = (ws_ref[grp] * mask).astype(_BF16)
            v_wide = jnp.concatenate(
                [vb[n * SGU_LEN:(n + 1) * SGU_LEN, cols] for n in range(n_chunks)], axis=1)
            s_wide = _dot(w, v_wide) + bs_ref[grp]
            s_pair.append(jnp.concatenate(
                [s_wide[:, n * SGU_GROUP_DIM:(n + 1) * SGU_GROUP_DIM]
                 for n in range(n_chunks)], axis=0))
        s = jnp.concatenate(s_pair, axis=1)
        acc = acc + _dot((u * s).astype(_BF16), w_out_ref[pcols, :])

    mix = _rms(acc, g[1:2]).reshape(streams, rows, D_MODEL)
    o_ref[...] = x + gt1 * mix


def _sgu_layer(x, mod, g_norm, w_in, g_v, w_s, mask, b_s, w_out, *, layer, streams, rows,
               emit_v):
    bsz, seq, _ = x.shape
    grid = (bsz // streams, seq // rows)
    tile = lambda d: pl.BlockSpec((streams, rows, d), lambda b, t: (b, t, 0))
    full = lambda a: pl.BlockSpec(a.shape, lambda b, t: (0,) * a.ndim)
    out_shape = [jax.ShapeDtypeStruct(x.shape, _F32)]
    out_specs = [tile(D_MODEL)]
    if emit_v:
        out_shape.append(jax.ShapeDtypeStruct((bsz, seq, D_SGU), _F32))
        out_specs.append(tile(D_SGU))
    return pl.pallas_call(
        functools.partial(_sgu_kernel, streams=streams, rows=rows),
        grid=grid,
        in_specs=[
            tile(D_MODEL),
            pl.BlockSpec((streams, N_MOD, D_MODEL), lambda b, t: (b, 0, 0)),
            full(g_norm), _layer_spec(w_in, layer), full(g_v), full(w_s), full(mask),
            full(b_s), _layer_spec(w_out, layer),
        ],
        out_specs=out_specs,
        out_shape=out_shape,
        compiler_params=pltpu.CompilerParams(
            dimension_semantics=("arbitrary", "arbitrary"),
            vmem_limit_bytes=VMEM_LIMIT_BYTES),
        name="sgu_mixer",
    )(x, mod, g_norm, w_in, g_v, w_s, mask, b_s, w_out)


def _ffn_kernel(x_ref, mod_ref, g_ref, w_up_ref, w_down_ref, o_ref, *, streams, rows):
    m = streams * rows
    x = x_ref[...]
    mod = mod_ref[...]
    sh2, sc2, gt2 = mod[:, 3:4], mod[:, 4:5], mod[:, 5:6]
    g = g_ref[...]
    h = _rms(x, g[2:3]) * (1.0 + sc2) + sh2
    hb = h.reshape(m, D_MODEL).astype(_BF16)
    acc = jnp.zeros((m, D_MODEL), _F32)
    for c in range(D_FF // FF_CHUNK):
        cols = slice(c * FF_CHUNK, (c + 1) * FF_CHUNK)
        a = jnp.maximum(_dot(hb, w_up_ref[:, cols]), 0.0)
        acc = acc + _dot((a * a).astype(_BF16), w_down_ref[cols, :])
    f = _rms(acc, g[3:4]).reshape(streams, rows, D_MODEL)
    o_ref[...] = x + gt2 * f


def _ffn_layer(x, mod, g_norm, w_up, w_down, *, layer, streams, rows):
    bsz, seq, _ = x.shape
    grid = (bsz // streams, seq // rows)
    tile = pl.BlockSpec((streams, rows, D_MODEL), lambda b, t: (b, t, 0))
    full = lambda a: pl.BlockSpec(a.shape, lambda b, t: (0,) * a.ndim)
    return pl.pallas_call(
        functools.partial(_ffn_kernel, streams=streams, rows=rows),
        grid=grid,
        in_specs=[
            tile,
            pl.BlockSpec((streams, N_MOD, D_MODEL), lambda b, t: (b, 0, 0)),
            full(g_norm), _layer_spec(w_up, layer), _layer_spec(w_down, layer),
        ],
        out_specs=tile,
        out_shape=jax.ShapeDtypeStruct(x.shape, _F32),
        compiler_params=pltpu.CompilerParams(
            dimension_semantics=("arbitrary", "arbitrary"),
            vmem_limit_bytes=VMEM_LIMIT_BYTES),
        name="channel_mlp",
    )(x, mod, g_norm, w_up, w_down)


def _window_sums(ext, win):
    tot, span = ext, 1
    while span < win:
        tot = tot + pltpu.roll(tot, span, axis=1)
        span *= 2
    return tot


def _zero_after(*vals):
    tot = None
    for v in vals:
        r = v.reshape(v.shape[0] // 8, 8, D_MODEL).sum(axis=0)
        for c in range(D_MODEL // 128):
            piece = r[:, c * 128:(c + 1) * 128]
            tot = piece if tot is None else tot + piece
    bits = pltpu.bitcast(tot, jnp.uint32)
    return pltpu.bitcast((bits >> 16) >> 16, _F32)


def _pool_mlp_kernel(x_ref, halo_ref, modp_ref, modq_ref, g_ref, wg_ref, scale_ref,
                     w_up_ref, w_down_ref, o_ref, tail_ref, hb_cur, hb_next, acc_cur,
                     acc_prev, x1_ring, carry_ref, *, streams, rows, tiles_per_stream,
                     n_tiles, hist, halo_is_h, skip_idle_main):
    by_rows = streams == 1
    sl_streams = streams if by_rows else streams // N_SLICES
    sl_rows = rows // N_SLICES if by_rows else rows
    sl_m = sl_streams * sl_rows
    s = pl.program_id(0)
    tile_pos = jnp.minimum(s, n_tiles - 1) % tiles_per_stream
    g = g_ref[...]

    def part(k):
        if by_rows:
            return (slice(None), slice(k * sl_rows, (k + 1) * sl_rows))
        return (slice(k * sl_streams, (k + 1) * sl_streams), slice(None))

    def part_streams(k):
        return slice(None) if by_rows else slice(k * sl_streams, (k + 1) * sl_streams)

    def flat(k):
        return slice(k * sl_m, (k + 1) * sl_m)

    @pl.when(s == 0)
    def _():
        hb_next[...] = jnp.zeros_like(hb_next)
        acc_cur[...] = jnp.zeros_like(acc_cur)
        x1_ring[...] = jnp.zeros_like(x1_ring)
        carry_ref[...] = jnp.zeros_like(carry_ref)

    def main(k):
        cols = slice(k * FF_CHUNK, (k + 1) * FF_CHUNK)
        a = jnp.maximum(_dot(hb_cur[...], w_up_ref[:, cols]), 0.0)
        d = _dot((a * a).astype(_BF16), w_down_ref[cols, :])
        acc_cur[...] = d if k == 0 else acc_cur[...] + d

    def pre(k):
        x = x_ref[part(k)]
        mod = modp_ref[part_streams(k)]
        sh1, sc1, gt1 = mod[:, 0:1], mod[:, 1:2], mod[:, 2:3]
        h = _rms(x, g[0:1]) * (1.0 + sc1) + sh1
        if halo_is_h:
            halo = halo_ref[part_streams(k)]
        else:
            halo = carry_ref[...]
            if k == 0:
                halo = jnp.where(tile_pos > 0, halo, 0.0)
        ext = jnp.concatenate([halo, h], axis=1)
        last = ext[:, sl_rows:, :]
        if halo_is_h:
            tail_ref[part_streams(k)] = last
        else:
            carry_ref[...] = last
            if k == N_SLICES - 1:
                tail_ref[...] = last
        base = tile_pos * rows + k * sl_rows if by_rows else 0
        pos = lax.broadcasted_iota(jnp.int32, (1, sl_rows, 1), 1) + (base + hist)
        outs = []
        for grp, win in enumerate(POOL_WINDOWS):
            cols = slice(grp * POOL_GROUP_DIM, (grp + 1) * POOL_GROUP_DIM)
            tot = _window_sums(ext[:, :, cols], win)[:, POOL_HALO:, :]
            inv_cnt = 1.0 / jnp.minimum(win, pos + 1).astype(_F32)
            pooled = (tot * inv_cnt - h[:, :, cols]).reshape(sl_m, POOL_GROUP_DIM)
            outs.append(_dot(pooled.astype(_BF16), wg_ref[grp]))
        y = jnp.concatenate(outs, axis=-1) * scale_ref[...]
        x = x + gt1 * _rms(y, g[1:2]).reshape(sl_streams, sl_rows, D_MODEL)
        x1_ring[s % 3, flat(k)] = x.reshape(sl_m, D_MODEL)
        sh2, sc2 = mod[:, 3:4], mod[:, 4:5]
        h2 = (_rms(x, g[2:3]) * (1.0 + sc2) + sh2).reshape(sl_m, D_MODEL)
        hb_next[flat(k)] = h2.astype(_BF16)
        return h2

    def post(k):
        gt2 = modq_ref[part_streams(k)][:, 5:6]
        x = x1_ring[(s + 1) % 3, flat(k)].reshape(sl_streams, sl_rows, D_MODEL)
        f = _rms(acc_prev[flat(k)], g[3:4]).reshape(sl_streams, sl_rows, D_MODEL)
        out = x + gt2 * f
        o_ref[part(k)] = out
        return out.reshape(sl_m, D_MODEL)

    hb_cur[...] = hb_next[...]
    acc_prev[...] = acc_cur[...]
    for k in range(N_SLICES):
        if skip_idle_main:
            pl.when(jnp.logical_and(s >= 1, s <= n_tiles))(functools.partial(main, k))
        else:
            main(k)
        done = _zero_after(pre(k), post(k))
        if k < N_SLICES - 1:
            acc_cur[0:8, 0:128] = acc_cur[0:8, 0:128] + done


def _pool_mlp_layer(x, halo, mod, g_norm, w_grp, scale, w_up, w_down, *, layer, streams,
                    rows, hist, halo_is_h):
    bsz, seq, _ = x.shape
    tiles_per_stream = seq // rows
    n_tiles = (bsz // streams) * tiles_per_stream
    m = streams * rows

    def cur(s):
        return jnp.minimum(s, n_tiles - 1)

    def done(s):
        return jnp.clip(s - 2, 0, n_tiles - 1)

    def at(which):
        return pl.BlockSpec(
            (streams, rows, D_MODEL),
            lambda s: (which(s) // tiles_per_stream, which(s) % tiles_per_stream, 0))

    def per_stream(which, n):
        return pl.BlockSpec((streams, n, D_MODEL),
                            lambda s: (which(s) // tiles_per_stream, 0, 0))

    def full(a):
        return pl.BlockSpec(a.shape, lambda s: (0,) * a.ndim,
                            pipeline_mode=pl.Buffered(1))

    return pl.pallas_call(
        functools.partial(_pool_mlp_kernel, streams=streams, rows=rows,
                          tiles_per_stream=tiles_per_stream, n_tiles=n_tiles, hist=hist,
                          halo_is_h=halo_is_h, skip_idle_main=n_tiles < MIN_PIPELINED_TILES),
        grid=(n_tiles + 2,),
        in_specs=[at(cur), per_stream(cur, POOL_HALO), per_stream(cur, N_MOD),
                  per_stream(done, N_MOD), full(g_norm), full(w_grp), full(scale),
                  _layer_spec(w_up, layer, pipeline_mode=pl.Buffered(1)),
                  _layer_spec(w_down, layer, pipeline_mode=pl.Buffered(1))],
        out_specs=[at(done), per_stream(cur, POOL_HALO)],
        out_shape=[jax.ShapeDtypeStruct(x.shape, _F32),
                   jax.ShapeDtypeStruct((bsz, POOL_HALO, D_MODEL), _F32)],
        scratch_shapes=[pltpu.VMEM((m, D_MODEL), _BF16), pltpu.VMEM((m, D_MODEL), _BF16),
                        pltpu.VMEM((m, D_MODEL), _F32), pltpu.VMEM((m, D_MODEL), _F32),
                        pltpu.VMEM((3, m, D_MODEL), _F32),
                        pltpu.VMEM((streams, POOL_HALO, D_MODEL), _F32)],
        compiler_params=pltpu.CompilerParams(
            dimension_semantics=("arbitrary",),
            vmem_limit_bytes=VMEM_LIMIT_BYTES),
        name="pool_mlp_block",
    )(x, halo, mod, mod, g_norm, w_grp, scale, w_up, w_down)


def _sgu_mask(length):
    pos = jnp.arange(SGU_LEN)
    same_chunk = (pos[:, None] // length) == (pos[None, :] // length)
    causal = (pos[None, :] % length) // CHUNK <= (pos[:, None] % length) // CHUNK
    return (same_chunk & causal).astype(_F32)


def _trunk(x, mod, pool_cache, weights, *, streams, rows):
    (g_norm, sgu_w_in, sgu_g_v, sgu_w_s, sgu_b_s, sgu_w_out, pool_w_grp, pool_scale,
     ffn_w_up, ffn_w_down) = weights
    sample = pool_cache is not None
    seq = x.shape[1]
    length = seq if sample else SGU_LEN
    reps = SGU_LEN // length
    mask = _sgu_mask(length)
    sgu_states, pool_states = [], []
    for i in range(DEPTH):
        j = i // 2
        if i % 2 == 0:
            w_s = jnp.tile(sgu_w_s[j][:, :length, :length], (1, reps, reps))
            b_s = jnp.tile(sgu_b_s[j][:, :length], (1, reps))[:, :, None]
            outs = _sgu_layer(x, mod[i], g_norm[i], sgu_w_in, sgu_g_v[j][None, :],
                              w_s, mask, b_s, sgu_w_out, layer=j, streams=streams,
                              rows=rows, emit_v=sample)
            x = outs[0]
            if sample:
                sgu_states.append(outs[1])
            x = _ffn_layer(x, mod[i], g_norm[i], ffn_w_up, ffn_w_down, layer=i,
                           streams=streams, rows=rows)
        else:
            if sample:
                halo = jnp.pad(pool_cache[j], ((0, 0), (1, 0), (0, 0)))
            else:
                halo = jnp.zeros((x.shape[0], POOL_HALO, D_MODEL), _F32)
            x, tail = _pool_mlp_layer(
                x, halo, mod[i], g_norm[i], pool_w_grp[j], pool_scale[j][None, :],
                ffn_w_up, ffn_w_down, layer=i, streams=streams, rows=rows,
                hist=POOL_STATE if sample else 0, halo_is_h=sample)
            pool_states.append(tail[:, 1:, :])
    return x, sgu_states, pool_states


def kernel(x_prompt, x_sample, cache_pool, c_prompt, c_sample, g_norm, w_ada, b_ada,
           sgu_w_in, sgu_g_v, sgu_w_s, sgu_b_s, sgu_w_out, pool_w_grp, pool_scale,
           ffn_w_up, ffn_w_down):
    n_p, n_s = c_prompt.shape[0], c_sample.shape[0]
    c_all = jnp.concatenate(
        [c_prompt, c_sample, jnp.zeros((MOD_ROWS - n_p - n_s, D_MODEL), _F32)], axis=0)
    mod = _modulation(c_all, w_ada, b_ada)
    mod = mod.reshape(DEPTH, MOD_ROWS, N_MOD, D_MODEL)
    mod_p, mod_s = mod[:, :n_p], mod[:, n_p:n_p + n_s]

    weights = (g_norm, sgu_w_in.astype(_BF16), sgu_g_v, sgu_w_s, sgu_b_s,
               sgu_w_out.astype(_BF16), pool_w_grp.astype(_BF16), pool_scale,
               ffn_w_up.astype(_BF16), ffn_w_down.astype(_BF16))

    y_p, _, pool_p = _trunk(x_prompt, mod_p, None, weights,
                            streams=1, rows=TILE_TOKENS)
    s_rows = x_sample.shape[1]
    y_s, sgu_s, pool_s = _trunk(x_sample, mod_s, cache_pool, weights,
                                streams=TILE_TOKENS // s_rows, rows=s_rows)
    return (y_p, y_s, jnp.stack(pool_p, axis=0), jnp.stack(sgu_s, axis=0),
            jnp.stack(pool_s, axis=0))
```

```python
import functools

import jax
import jax.numpy as jnp
from jax import lax
from jax.experimental import pallas as pl
from jax.experimental.pallas import tpu as pltpu

D_MODEL = 1024
DEPTH = 4
N_MOD = 6
EPS = 1e-6
D_SGU = 2 * D_MODEL
SGU_GROUPS = 8
SGU_GROUP_DIM = D_SGU // SGU_GROUPS
SGU_LEN = 128
SGU_PAIR = 2
CHUNK = 64
POOL_WINDOWS = (2, 4, 8, 16)
POOL_GROUP_DIM = D_MODEL // len(POOL_WINDOWS)
POOL_HALO = max(POOL_WINDOWS)
POOL_STATE = POOL_HALO - 1
D_FF = 4 * D_MODEL

TILE_TOKENS = 512
FF_CHUNK = 512
MOD_COLS = 1536
MOD_ROWS = 32
VMEM_LIMIT_BYTES = 56 * 1024 * 1024

_F32 = jnp.float32
_BF16 = jnp.bfloat16


def _dot(a, b):
    return jnp.dot(a, b, preferred_element_type=_F32)


def _rms(x, g):
    return x * lax.rsqrt(jnp.mean(x * x, axis=-1, keepdims=True) + EPS) * g


def _layer_spec(stacked, layer, **kwargs):
    index = (layer,) + (0,) * (stacked.ndim - 1)
    return pl.BlockSpec((None,) + stacked.shape[1:], lambda *_: index, **kwargs)


def _gelu_tanh(x):
    c = 0.7978845608028654
    half = 0.5 * x
    return half + half * jnp.tanh(x * (c + (c * 0.044715) * (x * x)))


def _mod_kernel(c_ref, w_ref, b_ref, o_ref):
    c = c_ref[...]
    s = (c * jax.nn.sigmoid(c)).astype(_BF16)
    o_ref[0] = _dot(s, w_ref[0].astype(_BF16)) + b_ref[0]


def _modulation(c_all, w_ada, b_ada):
    n_col = N_MOD * D_MODEL
    return pl.pallas_call(
        _mod_kernel,
        grid=(DEPTH, n_col // MOD_COLS),
        in_specs=[
            pl.BlockSpec((MOD_ROWS, D_MODEL), lambda i, j: (0, 0)),
            pl.BlockSpec((1, D_MODEL, MOD_COLS), lambda i, j: (i, 0, j)),
            pl.BlockSpec((1, 1, MOD_COLS), lambda i, j: (i, 0, j)),
        ],
        out_specs=pl.BlockSpec((1, MOD_ROWS, MOD_COLS), lambda i, j: (i, 0, j)),
        out_shape=jax.ShapeDtypeStruct((DEPTH, MOD_ROWS, n_col), _F32),
        compiler_params=pltpu.CompilerParams(
            dimension_semantics=("arbitrary", "arbitrary"),
            vmem_limit_bytes=VMEM_LIMIT_BYTES),
        name="adaln_mod",
    )(c_all, w_ada, b_ada.reshape(DEPTH, 1, n_col))


def _sgu_kernel(x_ref, mod_ref, g_ref, w_in_ref, gv_ref, ws_ref, mask_ref, bs_ref,
                w_out_ref, o_ref, *v_ref, streams, rows):
    m = streams * rows
    x = x_ref[...]
    mod = mod_ref[...]
    sh1, sc1, gt1 = mod[:, 0:1], mod[:, 1:2], mod[:, 2:3]
    g = g_ref[...]
    h = _rms(x, g[0:1]) * (1.0 + sc1) + sh1
    hb = h.reshape(m, D_MODEL).astype(_BF16)

    n_pairs = SGU_GROUPS // SGU_PAIR
    pair_cols = [slice(p * SGU_PAIR * SGU_GROUP_DIM, (p + 1) * SGU_PAIR * SGU_GROUP_DIM)
                 for p in range(n_pairs)]
    v_parts, u_parts = [], []
    for cols in pair_cols:
        vcols = slice(D_SGU + cols.start, D_SGU + cols.stop)
        v_parts.append(_gelu_tanh(_dot(hb, w_in_ref[:, vcols])))
        u_parts.append(_gelu_tanh(_dot(hb, w_in_ref[:, cols])))
    v = _rms(jnp.concatenate(v_parts, axis=1), gv_ref[...])
    if v_ref:
        v_ref[0][...] = v.reshape(streams, rows, D_SGU)
    vb = v.astype(_BF16)

    mask = mask_ref[...]
    n_chunks = m // SGU_LEN
    acc = jnp.zeros((m, D_MODEL), _F32)
    for pair in range(n_pairs):
        pcols = pair_cols[pair]
        u = u_parts[pair]
        s_pair = []
        for grp in range(pair * SGU_PAIR, (pair + 1) * SGU_PAIR):
            cols = slice(grp * SGU_GROUP_DIM, (grp + 1) * SGU_GROUP_DIM)
            w = (ws_ref[grp] * mask).astype(_BF16)
            v_wide = jnp.concatenate(
                [vb[n * SGU_LEN:(n + 1) * SGU_LEN, cols] for n in range(n_chunks)], axis=1)
            s_wide = _dot(w, v_wide) + bs_ref[grp]
            s_pair.append(jnp.concatenate(
                [s_wide[:, n * SGU_GROUP_DIM:(n + 1) * SGU_GROUP_DIM]
                 for n in range(n_chunks)], axis=0))
        s = jnp.concatenate(s_pair, axis=1)
        acc = acc + _dot((u * s).astype(_BF16), w_out_ref[pcols, :])

    mix = _rms(acc, g[1:2]).reshape(streams, rows, D_MODEL)
    o_ref[...] = x + gt1 * mix


def _sgu_layer(x, mod, g_norm, w_in, g_v, w_s, mask, b_s, w_out, *, layer, streams, rows,
               emit_v):
    bsz, seq, _ = x.shape
    grid = (bsz // streams, seq // rows)
    tile = lambda d: pl.BlockSpec((streams, rows, d), lambda b, t: (b, t, 0))
    full = lambda a: pl.BlockSpec(a.shape, lambda b, t: (0,) * a.ndim)
    out_shape = [jax.ShapeDtypeStruct(x.shape, _F32)]
    out_specs = [tile(D_MODEL)]
    if emit_v:
        out_shape.append(jax.ShapeDtypeStruct((bsz, seq, D_SGU), _F32))
        out_specs.append(tile(D_SGU))
    return pl.pallas_call(
        functools.partial(_sgu_kernel, streams=streams, rows=rows),
        grid=grid,
        in_specs=[
            tile(D_MODEL),
            pl.BlockSpec((streams, N_MOD, D_MODEL), lambda b, t: (b, 0, 0)),
            full(g_norm), _layer_spec(w_in, layer), full(g_v), full(w_s), full(mask),
            full(b_s), _layer_spec(w_out, layer),
        ],
        out_specs=out_specs,
        out_shape=out_shape,
        compiler_params=pltpu.CompilerParams(
            dimension_semantics=("arbitrary", "arbitrary"),
            vmem_limit_bytes=VMEM_LIMIT_BYTES),
        name="sgu_mixer",
    )(x, mod, g_norm, w_in, g_v, w_s, mask, b_s, w_out)


def _ffn_body(x, mod, g, w_up_ref, w_down_ref):
    streams, rows, _ = x.shape
    m = streams * rows
    sh2, sc2, gt2 = mod[:, 3:4], mod[:, 4:5], mod[:, 5:6]
    h = _rms(x, g[2:3]) * (1.0 + sc2) + sh2
    hb = h.reshape(m, D_MODEL).astype(_BF16)
    acc = jnp.zeros((m, D_MODEL), _F32)
    for c in range(D_FF // FF_CHUNK):
        cols = slice(c * FF_CHUNK, (c + 1) * FF_CHUNK)
        a = jnp.maximum(_dot(hb, w_up_ref[:, cols]), 0.0)
        acc = acc + _dot((a * a).astype(_BF16), w_down_ref[cols, :])
    f = _rms(acc, g[3:4]).reshape(streams, rows, D_MODEL)
    return x + gt2 * f


def _ffn_kernel(x_ref, mod_ref, g_ref, w_up_ref, w_down_ref, o_ref):
    o_ref[...] = _ffn_body(x_ref[...], mod_ref[...], g_ref[...], w_up_ref, w_down_ref)


def _ffn_layer(x, mod, g_norm, w_up, w_down, *, layer, streams, rows):
    bsz, seq, _ = x.shape
    grid = (bsz // streams, seq // rows)
    tile = pl.BlockSpec((streams, rows, D_MODEL), lambda b, t: (b, t, 0))
    full = lambda a: pl.BlockSpec(a.shape, lambda b, t: (0,) * a.ndim)
    return pl.pallas_call(
        _ffn_kernel,
        grid=grid,
        in_specs=[
            tile,
            pl.BlockSpec((streams, N_MOD, D_MODEL), lambda b, t: (b, 0, 0)),
            full(g_norm), _layer_spec(w_up, layer), _layer_spec(w_down, layer),
        ],
        out_specs=tile,
        out_shape=jax.ShapeDtypeStruct(x.shape, _F32),
        compiler_params=pltpu.CompilerParams(
            dimension_semantics=("arbitrary", "arbitrary"),
            vmem_limit_bytes=VMEM_LIMIT_BYTES),
        name="channel_mlp",
    )(x, mod, g_norm, w_up, w_down)


def _window_sums(ext, win):
    tot, span = ext, 1
    while span < win:
        tot = tot + pltpu.roll(tot, span, axis=1)
        span *= 2
    return tot


def _pool_ffn_kernel(*refs, hist, halo_is_h):
    if halo_is_h:
        (x_ref, halo_ref, mod_ref, g_ref, wg_ref, scale_ref, w_up_ref, w_down_ref,
         o_ref, tail_ref) = refs
    else:
        (x_ref, mod_ref, g_ref, wg_ref, scale_ref, w_up_ref, w_down_ref,
         o_ref, tail_ref, carry_ref) = refs
    t = pl.program_id(1)
    x = x_ref[...]
    streams, rows, _ = x.shape
    mod = mod_ref[...]
    sh1, sc1, gt1 = mod[:, 0:1], mod[:, 1:2], mod[:, 2:3]
    g = g_ref[...]

    h = _rms(x, g[0:1]) * (1.0 + sc1) + sh1
    if halo_is_h:
        halo = halo_ref[...]
    else:
        @pl.when(t == 0)
        def _():
            carry_ref[...] = jnp.zeros_like(carry_ref)
        halo = carry_ref[...]
    ext = jnp.concatenate([halo, h], axis=1)
    last = ext[:, rows:, :]
    tail_ref[...] = last
    if not halo_is_h:
        carry_ref[...] = last
    pos = lax.broadcasted_iota(jnp.int32, (1, rows, 1), 1) + (t * rows + hist)
    outs = []
    for grp, win in enumerate(POOL_WINDOWS):
        cols = slice(grp * POOL_GROUP_DIM, (grp + 1) * POOL_GROUP_DIM)
        tot = _window_sums(ext[:, :, cols], win)[:, POOL_HALO:, :]
        inv_cnt = 1.0 / jnp.minimum(win, pos + 1).astype(_F32)
        pooled = (tot * inv_cnt - h[:, :, cols]).reshape(streams * rows, POOL_GROUP_DIM)
        outs.append(_dot(pooled.astype(_BF16), wg_ref[grp]))
    y = jnp.concatenate(outs, axis=-1) * scale_ref[...]
    x = x + gt1 * _rms(y, g[1:2]).reshape(streams, rows, D_MODEL)
    o_ref[...] = _ffn_body(x, mod, g, w_up_ref, w_down_ref)


def _pool_ffn_layer(x, halo, mod, g_norm, w_grp, scale, w_up, w_down, *, layer, streams,
                    rows, hist):
    bsz, seq, _ = x.shape
    grid = (bsz // streams, seq // rows)
    halo_is_h = halo is not None
    tile = pl.BlockSpec((streams, rows, D_MODEL), lambda b, t: (b, t, 0))
    per_stream = lambda n: pl.BlockSpec((streams, n, D_MODEL), lambda b, t: (b, 0, 0))
    full = lambda a: pl.BlockSpec(a.shape, lambda b, t: (0,) * a.ndim)
    operands = [x] + ([halo] if halo_is_h else []) + [mod, g_norm, w_grp, scale, w_up, w_down]
    in_specs = ([tile] + ([per_stream(POOL_HALO)] if halo_is_h else [])
                + [per_stream(N_MOD), full(g_norm), full(w_grp), full(scale),
                   _layer_spec(w_up, layer), _layer_spec(w_down, layer)])
    scratch = [] if halo_is_h else [pltpu.VMEM((streams, POOL_HALO, D_MODEL), _F32)]
    return pl.pallas_call(
        functools.partial(_pool_ffn_kernel, hist=hist, halo_is_h=halo_is_h),
        grid=grid,
        in_specs=in_specs,
        out_specs=[tile, per_stream(POOL_HALO)],
        out_shape=[jax.ShapeDtypeStruct(x.shape, _F32),
                   jax.ShapeDtypeStruct((bsz, POOL_HALO, D_MODEL), _F32)],
        scratch_shapes=scratch,
        compiler_params=pltpu.CompilerParams(
            dimension_semantics=("arbitrary", "arbitrary"),
            vmem_limit_bytes=VMEM_LIMIT_BYTES),
        name="pool_mlp",
    )(*operands)


def _sgu_mask(length):
    pos = jnp.arange(SGU_LEN)
    same_chunk = (pos[:, None] // length) == (pos[None, :] // length)
    causal = (pos[None, :] % length) // CHUNK <= (pos[:, None] % length) // CHUNK
    return (same_chunk & causal).astype(_F32)


def _trunk(x, mod, pool_cache, weights, *, streams, rows):
    (g_norm, sgu_w_in, sgu_g_v, sgu_w_s, sgu_b_s, sgu_w_out, pool_w_grp, pool_scale,
     ffn_w_up, ffn_w_down) = weights
    sample = pool_cache is not None
    seq = x.shape[1]
    length = seq if sample else SGU_LEN
    reps = SGU_LEN // length
    mask = _sgu_mask(length)
    sgu_states, pool_states = [], []
    for i in range(DEPTH):
        j = i // 2
        if i % 2 == 0:
            w_s = jnp.tile(sgu_w_s[j][:, :length, :length], (1, reps, reps))
            b_s = jnp.tile(sgu_b_s[j][:, :length], (1, reps))[:, :, None]
            outs = _sgu_layer(x, mod[i], g_norm[i], sgu_w_in, sgu_g_v[j][None, :],
                              w_s, mask, b_s, sgu_w_out, layer=j, streams=streams,
                              rows=rows, emit_v=sample)
            x = outs[0]
            if sample:
                sgu_states.append(outs[1])
            x = _ffn_layer(x, mod[i], g_norm[i], ffn_w_up, ffn_w_down, layer=i,
                           streams=streams, rows=rows)
        else:
            halo = jnp.pad(pool_cache[j], ((0, 0), (1, 0), (0, 0))) if sample else None
            x, tail = _pool_ffn_layer(
                x, halo, mod[i], g_norm[i], pool_w_grp[j], pool_scale[j][None, :],
                ffn_w_up, ffn_w_down, layer=i, streams=streams, rows=rows,
                hist=POOL_STATE if sample else 0)
            pool_states.append(tail[:, 1:, :])
    return x, sgu_states, pool_states


def kernel(x_prompt, x_sample, cache_pool, c_prompt, c_sample, g_norm, w_ada, b_ada,
           sgu_w_in, sgu_g_v, sgu_w_s, sgu_b_s, sgu_w_out, pool_w_grp, pool_scale,
           ffn_w_up, ffn_w_down):
    n_p, n_s = c_prompt.shape[0], c_sample.shape[0]
    c_all = jnp.concatenate(
        [c_prompt, c_sample, jnp.zeros((MOD_ROWS - n_p - n_s, D_MODEL), _F32)], axis=0)
    mod = _modulation(c_all, w_ada, b_ada)
    mod = mod.reshape(DEPTH, MOD_ROWS, N_MOD, D_MODEL)
    mod_p, mod_s = mod[:, :n_p], mod[:, n_p:n_p + n_s]

    weights = (g_norm, sgu_w_in.astype(_BF16), sgu_g_v, sgu_w_s, sgu_b_s,
               sgu_w_out.astype(_BF16), pool_w_grp.astype(_BF16), pool_scale,
               ffn_w_up.astype(_BF16), ffn_w_down.astype(_BF16))

    y_p, _, pool_p = _trunk(x_prompt, mod_p, None, weights,
                            streams=1, rows=TILE_TOKENS)
    s_rows = x_sample.shape[1]
    y_s, sgu_s, pool_s = _trunk(x_sample, mod_s, cache_pool, weights,
                                streams=TILE_TOKENS // s_rows, rows=s_rows)
    return (y_p, y_s, jnp.stack(pool_p, axis=0), jnp.stack(sgu_s, axis=0),
            jnp.stack(pool_s, axis=0))
```

```python
import functools

import jax
import jax.numpy as jnp
from jax import lax
from jax.experimental import pallas as pl
from jax.experimental.pallas import tpu as pltpu

D_MODEL = 1024
DEPTH = 4
N_MOD = 6
EPS = 1e-6
D_SGU = 2 * D_MODEL
SGU_GROUPS = 8
SGU_GROUP_DIM = D_SGU // SGU_GROUPS
SGU_LEN = 128
SGU_PAIR = 2
CHUNK = 64
POOL_WINDOWS = (2, 4, 8, 16)
POOL_GROUP_DIM = D_MODEL // len(POOL_WINDOWS)
POOL_HALO = max(POOL_WINDOWS)
POOL_STATE = POOL_HALO - 1
D_FF = 4 * D_MODEL

TILE_TOKENS = 512
FFN_TILE_TOKENS = 1024
FF_CHUNK = 512
MOD_COLS = 1536
MOD_ROWS = 32
VMEM_LIMIT_BYTES = 56 * 1024 * 1024

_F32 = jnp.float32
_BF16 = jnp.bfloat16


def _dot(a, b):
    return jnp.dot(a, b, preferred_element_type=_F32)


def _rms(x, gain):
    return (x * gain) * lax.rsqrt(jnp.mean(x * x, axis=-1, keepdims=True) + EPS)


def _layer_spec(stacked, layer, **kwargs):
    index = (layer,) + (0,) * (stacked.ndim - 1)
    return pl.BlockSpec((None,) + stacked.shape[1:], lambda *_: index, **kwargs)


def _gelu_tanh(x):
    c = 0.7978845608028654
    half = 0.5 * x
    return half + half * jnp.tanh(x * (c + (c * 0.044715) * (x * x)))


def _mod_kernel(c_ref, w_ref, b_ref, o_ref):
    c = c_ref[...]
    s = (c * jax.nn.sigmoid(c)).astype(_BF16)
    o_ref[0] = _dot(s, w_ref[0].astype(_BF16)) + b_ref[0]


def _modulation(c_all, w_ada, b_ada):
    n_col = N_MOD * D_MODEL
    return pl.pallas_call(
        _mod_kernel,
        grid=(DEPTH, n_col // MOD_COLS),
        in_specs=[
            pl.BlockSpec((MOD_ROWS, D_MODEL), lambda i, j: (0, 0)),
            pl.BlockSpec((1, D_MODEL, MOD_COLS), lambda i, j: (i, 0, j)),
            pl.BlockSpec((1, 1, MOD_COLS), lambda i, j: (i, 0, j)),
        ],
        out_specs=pl.BlockSpec((1, MOD_ROWS, MOD_COLS), lambda i, j: (i, 0, j)),
        out_shape=jax.ShapeDtypeStruct((DEPTH, MOD_ROWS, n_col), _F32),
        compiler_params=pltpu.CompilerParams(
            dimension_semantics=("arbitrary", "arbitrary"),
            vmem_limit_bytes=VMEM_LIMIT_BYTES),
        name="adaln_mod",
    )(c_all, w_ada, b_ada.reshape(DEPTH, 1, n_col))


def _sgu_kernel(x_ref, mod_ref, g_ref, w_in_ref, gv_ref, ws_ref, mask_ref, bs_ref,
                w_out_ref, o_ref, *v_ref, streams, rows):
    m = streams * rows
    x = x_ref[...]
    mod = mod_ref[...]
    sh1, sc1, gt1 = mod[:, 0:1], mod[:, 1:2], mod[:, 2:3]
    g = g_ref[...]
    h = _rms(x, g[0:1] * (1.0 + sc1)) + sh1
    hb = h.reshape(m, D_MODEL).astype(_BF16)

    n_pairs = SGU_GROUPS // SGU_PAIR
    pair_cols = [slice(p * SGU_PAIR * SGU_GROUP_DIM, (p + 1) * SGU_PAIR * SGU_GROUP_DIM)
                 for p in range(n_pairs)]
    v_parts, u_parts = [], []
    for cols in pair_cols:
        vcols = slice(D_SGU + cols.start, D_SGU + cols.stop)
        v_parts.append(_gelu_tanh(_dot(hb, w_in_ref[:, vcols])))
        u_parts.append(_gelu_tanh(_dot(hb, w_in_ref[:, cols])))
    v = _rms(jnp.concatenate(v_parts, axis=1), gv_ref[...])
    if v_ref:
        v_ref[0][...] = v.reshape(streams, rows, D_SGU)
    vb = v.astype(_BF16)

    mask = mask_ref[...]
    n_chunks = m // SGU_LEN
    acc = jnp.zeros((m, D_MODEL), _F32)
    for pair in range(n_pairs):
        pcols = pair_cols[pair]
        u = u_parts[pair]
        s_pair = []
        for grp in range(pair * SGU_PAIR, (pair + 1) * SGU_PAIR):
            cols = slice(grp * SGU_GROUP_DIM, (grp + 1) * SGU_GROUP_DIM)
            w = (ws_ref[grp] * mask).astype(_BF16)
            v_wide = jnp.concatenate(
                [vb[n * SGU_LEN:(n + 1) * SGU_LEN, cols] for n in range(n_chunks)], axis=1)
            s_wide = _dot(w, v_wide) + bs_ref[grp]
            s_pair.append(jnp.concatenate(
                [s_wide[:, n * SGU_GROUP_DIM:(n + 1) * SGU_GROUP_DIM]
                 for n in range(n_chunks)], axis=0))
        s = jnp.concatenate(s_pair, axis=1)
        acc = acc + _dot((u * s).astype(_BF16), w_out_ref[pcols, :])

    o_ref[...] = x + _rms(acc.reshape(streams, rows, D_MODEL), g[1:2] * gt1)


def _sgu_layer(x, mod, g_norm, w_in, g_v, w_s, mask, b_s, w_out, *, layer, streams, rows,
               emit_v):
    bsz, seq, _ = x.shape
    grid = (bsz // streams, seq // rows)
    tile = lambda d: pl.BlockSpec((streams, rows, d), lambda b, t: (b, t, 0))
    full = lambda a: pl.BlockSpec(a.shape, lambda b, t: (0,) * a.ndim)
    out_shape = [jax.ShapeDtypeStruct(x.shape, _F32)]
    out_specs = [tile(D_MODEL)]
    if emit_v:
        out_shape.append(jax.ShapeDtypeStruct((bsz, seq, D_SGU), _F32))
        out_specs.append(tile(D_SGU))
    return pl.pallas_call(
        functools.partial(_sgu_kernel, streams=streams, rows=rows),
        grid=grid,
        in_specs=[
            tile(D_MODEL),
            pl.BlockSpec((streams, N_MOD, D_MODEL), lambda b, t: (b, 0, 0)),
            full(g_norm), _layer_spec(w_in, layer), full(g_v), full(w_s), full(mask),
            full(b_s), _layer_spec(w_out, layer),
        ],
        out_specs=out_specs,
        out_shape=out_shape,
        compiler_params=pltpu.CompilerParams(
            dimension_semantics=("arbitrary", "arbitrary"),
            vmem_limit_bytes=VMEM_LIMIT_BYTES),
        name="sgu_mixer",
    )(x, mod, g_norm, w_in, g_v, w_s, mask, b_s, w_out)


def _ffn_body(x, mod, g, w_up_ref, w_down_ref):
    streams, rows, _ = x.shape
    m = streams * rows
    sh2, sc2, gt2 = mod[:, 3:4], mod[:, 4:5], mod[:, 5:6]
    h = _rms(x, g[2:3] * (1.0 + sc2)) + sh2
    hb = h.reshape(m, D_MODEL).astype(_BF16)
    acc = jnp.zeros((m, D_MODEL), _F32)
    for c in range(D_FF // FF_CHUNK):
        cols = slice(c * FF_CHUNK, (c + 1) * FF_CHUNK)
        a = jnp.maximum(_dot(hb, w_up_ref[:, cols]), 0.0)
        acc = acc + _dot((a * a).astype(_BF16), w_down_ref[cols, :])
    return x + _rms(acc.reshape(streams, rows, D_MODEL), g[3:4] * gt2)


def _ffn_kernel(x_ref, mod_ref, g_ref, w_up_ref, w_down_ref, o_ref):
    o_ref[...] = _ffn_body(x_ref[...], mod_ref[...], g_ref[...], w_up_ref, w_down_ref)


def _ffn_layer(x, mod, g_norm, w_up, w_down, *, layer, streams, rows):
    bsz, seq, _ = x.shape
    grid = (bsz // streams, seq // rows)
    tile = pl.BlockSpec((streams, rows, D_MODEL), lambda b, t: (b, t, 0))
    full = lambda a: pl.BlockSpec(a.shape, lambda b, t: (0,) * a.ndim)
    return pl.pallas_call(
        _ffn_kernel,
        grid=grid,
        in_specs=[
            tile,
            pl.BlockSpec((streams, N_MOD, D_MODEL), lambda b, t: (b, 0, 0)),
            full(g_norm), _layer_spec(w_up, layer, pipeline_mode=pl.Buffered(1)),
            _layer_spec(w_down, layer, pipeline_mode=pl.Buffered(1)),
        ],
        out_specs=tile,
        out_shape=jax.ShapeDtypeStruct(x.shape, _F32),
        compiler_params=pltpu.CompilerParams(
            dimension_semantics=("arbitrary", "arbitrary"),
            vmem_limit_bytes=VMEM_LIMIT_BYTES),
        name="channel_mlp",
    )(x, mod, g_norm, w_up, w_down)


def _window_sums(ext, win):
    tot, span = ext, 1
    while span < win:
        tot = tot + pltpu.roll(tot, span, axis=1)
        span *= 2
    return tot


def _pool_ffn_kernel(*refs, hist, halo_is_h):
    if halo_is_h:
        (x_ref, halo_ref, mod_ref, g_ref, wg_ref, scale_ref, w_up_ref, w_down_ref,
         o_ref, tail_ref) = refs
    else:
        (x_ref, mod_ref, g_ref, wg_ref, scale_ref, w_up_ref, w_down_ref,
         o_ref, tail_ref, carry_ref) = refs
    t = pl.program_id(1)
    x = x_ref[...]
    streams, rows, _ = x.shape
    mod = mod_ref[...]
    sh1, sc1, gt1 = mod[:, 0:1], mod[:, 1:2], mod[:, 2:3]
    g = g_ref[...]

    h = _rms(x, g[0:1] * (1.0 + sc1)) + sh1
    if halo_is_h:
        halo = halo_ref[...]
    else:
        @pl.when(t == 0)
        def _():
            carry_ref[...] = jnp.zeros_like(carry_ref)
        halo = carry_ref[...]
    ext = jnp.concatenate([halo, h], axis=1)
    last = ext[:, rows:, :]
    tail_ref[...] = last
    if not halo_is_h:
        carry_ref[...] = last
    pos = lax.broadcasted_iota(jnp.int32, (1, rows, 1), 1) + (t * rows + hist)
    outs = []
    for grp, win in enumerate(POOL_WINDOWS):
        cols = slice(grp * POOL_GROUP_DIM, (grp + 1) * POOL_GROUP_DIM)
        tot = _window_sums(ext[:, :, cols], win)[:, POOL_HALO:, :]
        inv_cnt = 1.0 / jnp.minimum(win, pos + 1).astype(_F32)
        pooled = (tot * inv_cnt - h[:, :, cols]).reshape(streams * rows, POOL_GROUP_DIM)
        outs.append(_dot(pooled.astype(_BF16), wg_ref[grp]))
    y = (jnp.concatenate(outs, axis=-1) * scale_ref[...]).reshape(streams, rows, D_MODEL)
    x = x + _rms(y, g[1:2] * gt1)
    o_ref[...] = _ffn_body(x, mod, g, w_up_ref, w_down_ref)


def _pool_ffn_layer(x, halo, mod, g_norm, w_grp, scale, w_up, w_down, *, layer, streams,
                    rows, hist):
    bsz, seq, _ = x.shape
    grid = (bsz // streams, seq // rows)
    halo_is_h = halo is not None
    tile = pl.BlockSpec((streams, rows, D_MODEL), lambda b, t: (b, t, 0))
    per_stream = lambda n: pl.BlockSpec((streams, n, D_MODEL), lambda b, t: (b, 0, 0))
    full = lambda a: pl.BlockSpec(a.shape, lambda b, t: (0,) * a.ndim)
    operands = [x] + ([halo] if halo_is_h else []) + [mod, g_norm, w_grp, scale, w_up, w_down]
    in_specs = ([tile] + ([per_stream(POOL_HALO)] if halo_is_h else [])
                + [per_stream(N_MOD), full(g_norm), full(w_grp), full(scale),
                   _layer_spec(w_up, layer, pipeline_mode=pl.Buffered(1)),
                   _layer_spec(w_down, layer, pipeline_mode=pl.Buffered(1))])
    scratch = [] if halo_is_h else [pltpu.VMEM((streams, POOL_HALO, D_MODEL), _F32)]
    return pl.pallas_call(
        functools.partial(_pool_ffn_kernel, hist=hist, halo_is_h=halo_is_h),
        grid=grid,
        in_specs=in_specs,
        out_specs=[tile, per_stream(POOL_HALO)],
        out_shape=[jax.ShapeDtypeStruct(x.shape, _F32),
                   jax.ShapeDtypeStruct((bsz, POOL_HALO, D_MODEL), _F32)],
        scratch_shapes=scratch,
        compiler_params=pltpu.CompilerParams(
            dimension_semantics=("arbitrary", "arbitrary"),
            vmem_limit_bytes=VMEM_LIMIT_BYTES),
        name="pool_mlp",
    )(*operands)


def _sgu_mask(length):
    pos = jnp.arange(SGU_LEN)
    same_chunk = (pos[:, None] // length) == (pos[None, :] // length)
    causal = (pos[None, :] % length) // CHUNK <= (pos[:, None] % length) // CHUNK
    return (same_chunk & causal).astype(_F32)


def _trunk(x, mod, pool_cache, weights, *, streams, rows, ffn_rows):
    (g_norm, sgu_w_in, sgu_g_v, sgu_w_s, sgu_b_s, sgu_w_out, pool_w_grp, pool_scale,
     ffn_w_up, ffn_w_down) = weights
    sample = pool_cache is not None
    seq = x.shape[1]
    length = seq if sample else SGU_LEN
    reps = SGU_LEN // length
    mask = _sgu_mask(length)
    sgu_states, pool_states = [], []
    for i in range(DEPTH):
        j = i // 2
        if i % 2 == 0:
            w_s = jnp.tile(sgu_w_s[j][:, :length, :length], (1, reps, reps))
            b_s = jnp.tile(sgu_b_s[j][:, :length], (1, reps))[:, :, None]
            outs = _sgu_layer(x, mod[i], g_norm[i], sgu_w_in, sgu_g_v[j][None, :],
                              w_s, mask, b_s, sgu_w_out, layer=j, streams=streams,
                              rows=rows, emit_v=sample)
            x = outs[0]
            if sample:
                sgu_states.append(outs[1])
            x = _ffn_layer(x, mod[i], g_norm[i], ffn_w_up, ffn_w_down, layer=i,
                           streams=streams, rows=ffn_rows)
        else:
            halo = jnp.pad(pool_cache[j], ((0, 0), (1, 0), (0, 0))) if sample else None
            x, tail = _pool_ffn_layer(
                x, halo, mod[i], g_norm[i], pool_w_grp[j], pool_scale[j][None, :],
                ffn_w_up, ffn_w_down, layer=i, streams=streams, rows=ffn_rows,
                hist=POOL_STATE if sample else 0)
            pool_states.append(tail[:, 1:, :])
    return x, sgu_states, pool_states


def kernel(x_prompt, x_sample, cache_pool, c_prompt, c_sample, g_norm, w_ada, b_ada,
           sgu_w_in, sgu_g_v, sgu_w_s, sgu_b_s, sgu_w_out, pool_w_grp, pool_scale,
           ffn_w_up, ffn_w_down):
    n_p, n_s = c_prompt.shape[0], c_sample.shape[0]
    c_all = jnp.concatenate(
        [c_prompt, c_sample, jnp.zeros((MOD_ROWS - n_p - n_s, D_MODEL), _F32)], axis=0)
    mod = _modulation(c_all, w_ada, b_ada)
    mod = mod.reshape(DEPTH, MOD_ROWS, N_MOD, D_MODEL)
    mod_p, mod_s = mod[:, :n_p], mod[:, n_p:n_p + n_s]

    weights = (g_norm, sgu_w_in.astype(_BF16), sgu_g_v, sgu_w_s, sgu_b_s,
               sgu_w_out.astype(_BF16), pool_w_grp.astype(_BF16), pool_scale,
               ffn_w_up.astype(_BF16), ffn_w_down.astype(_BF16))

    y_p, _, pool_p = _trunk(x_prompt, mod_p, None, weights,
                            streams=1, rows=TILE_TOKENS, ffn_rows=FFN_TILE_TOKENS)
    s_rows = x_sample.shape[1]
    y_s, sgu_s, pool_s = _trunk(x_sample, mod_s, cache_pool, weights,
                                streams=TILE_TOKENS // s_rows, rows=s_rows, ffn_rows=s_rows)
    return (y_p, y_s, jnp.stack(pool_p, axis=0), jnp.stack(sgu_s, axis=0),
            jnp.stack(pool_s, axis=0))
```

```python
import functools

import jax
import jax.numpy as jnp
from jax import lax
from jax.experimental import pallas as pl
from jax.experimental.pallas import tpu as pltpu

D_MODEL = 1024
DEPTH = 4
N_MOD = 6
EPS = 1e-6
D_SGU = 2 * D_MODEL
SGU_GROUPS = 8
SGU_GROUP_DIM = D_SGU // SGU_GROUPS
SGU_LEN = 128
SGU_PAIR = 2
CHUNK = 64
POOL_WINDOWS = (2, 4, 8, 16)
POOL_GROUP_DIM = D_MODEL // len(POOL_WINDOWS)
POOL_HALO = max(POOL_WINDOWS)
POOL_STATE = POOL_HALO - 1
D_FF = 4 * D_MODEL

TILE_TOKENS = 512
FFN_TILE_TOKENS = 1024
FF_CHUNK = 512
MOD_COLS = 1536
MOD_ROWS = 32
VMEM_LIMIT_BYTES = 56 * 1024 * 1024

_F32 = jnp.float32
_BF16 = jnp.bfloat16


def _dot(a, b):
    return jnp.dot(a, b, preferred_element_type=_F32)


def _rms(x, gain, gain_first=True):
    inv = lax.rsqrt(jnp.mean(x * x, axis=-1, keepdims=True) + EPS)
    return (x * gain) * inv if gain_first else x * inv * gain


def _layer_spec(stacked, layer, **kwargs):
    index = (layer,) + (0,) * (stacked.ndim - 1)
    return pl.BlockSpec((None,) + stacked.shape[1:], lambda *_: index, **kwargs)


def _gelu_tanh(x):
    c = 0.7978845608028654
    half = 0.5 * x
    return half + half * jnp.tanh(x * (c + (c * 0.044715) * (x * x)))


def _mod_kernel(c_ref, w_ref, b_ref, o_ref):
    c = c_ref[...]
    s = (c * jax.nn.sigmoid(c)).astype(_BF16)
    o_ref[0] = _dot(s, w_ref[0].astype(_BF16)) + b_ref[0]


def _modulation(c_all, w_ada, b_ada):
    n_col = N_MOD * D_MODEL
    return pl.pallas_call(
        _mod_kernel,
        grid=(DEPTH, n_col // MOD_COLS),
        in_specs=[
            pl.BlockSpec((MOD_ROWS, D_MODEL), lambda i, j: (0, 0)),
            pl.BlockSpec((1, D_MODEL, MOD_COLS), lambda i, j: (i, 0, j)),
            pl.BlockSpec((1, 1, MOD_COLS), lambda i, j: (i, 0, j)),
        ],
        out_specs=pl.BlockSpec((1, MOD_ROWS, MOD_COLS), lambda i, j: (i, 0, j)),
        out_shape=jax.ShapeDtypeStruct((DEPTH, MOD_ROWS, n_col), _F32),
        compiler_params=pltpu.CompilerParams(
            dimension_semantics=("arbitrary", "arbitrary"),
            vmem_limit_bytes=VMEM_LIMIT_BYTES),
        name="adaln_mod",
    )(c_all, w_ada, b_ada.reshape(DEPTH, 1, n_col))


def _sgu_kernel(x_ref, mod_ref, g_ref, w_in_ref, gv_ref, ws_ref, mask_ref, bs_ref,
                w_out_ref, o_ref, *v_ref, streams, rows):
    m = streams * rows
    x = x_ref[...]
    mod = mod_ref[...]
    sh1, sc1, gt1 = mod[:, 0:1], mod[:, 1:2], mod[:, 2:3]
    g = g_ref[...]
    h = _rms(x, g[0:1], gain_first=False) * (1.0 + sc1) + sh1
    hb = h.reshape(m, D_MODEL).astype(_BF16)

    n_pairs = SGU_GROUPS // SGU_PAIR
    pair_cols = [slice(p * SGU_PAIR * SGU_GROUP_DIM, (p + 1) * SGU_PAIR * SGU_GROUP_DIM)
                 for p in range(n_pairs)]
    v_parts, u_parts = [], []
    for cols in pair_cols:
        vcols = slice(D_SGU + cols.start, D_SGU + cols.stop)
        v_parts.append(_gelu_tanh(_dot(hb, w_in_ref[:, vcols])))
        u_parts.append(_gelu_tanh(_dot(hb, w_in_ref[:, cols])))
    v = _rms(jnp.concatenate(v_parts, axis=1), gv_ref[...], gain_first=False)
    if v_ref:
        v_ref[0][...] = v.reshape(streams, rows, D_SGU)
    vb = v.astype(_BF16)

    mask = mask_ref[...]
    n_chunks = m // SGU_LEN
    acc = jnp.zeros((m, D_MODEL), _F32)
    for pair in range(n_pairs):
        pcols = pair_cols[pair]
        u = u_parts[pair]
        s_pair = []
        for grp in range(pair * SGU_PAIR, (pair + 1) * SGU_PAIR):
            cols = slice(grp * SGU_GROUP_DIM, (grp + 1) * SGU_GROUP_DIM)
            w = (ws_ref[grp] * mask).astype(_BF16)
            v_wide = jnp.concatenate(
                [vb[n * SGU_LEN:(n + 1) * SGU_LEN, cols] for n in range(n_chunks)], axis=1)
            s_wide = _dot(w, v_wide) + bs_ref[grp]
            s_pair.append(jnp.concatenate(
                [s_wide[:, n * SGU_GROUP_DIM:(n + 1) * SGU_GROUP_DIM]
                 for n in range(n_chunks)], axis=0))
        s = jnp.concatenate(s_pair, axis=1)
        acc = acc + _dot((u * s).astype(_BF16), w_out_ref[pcols, :])

    mix = _rms(acc, g[1:2], gain_first=False).reshape(streams, rows, D_MODEL)
    o_ref[...] = x + gt1 * mix


def _sgu_layer(x, mod, g_norm, w_in, g_v, w_s, mask, b_s, w_out, *, layer, streams, rows,
               emit_v):
    bsz, seq, _ = x.shape
    grid = (bsz // streams, seq // rows)
    tile = lambda d: pl.BlockSpec((streams, rows, d), lambda b, t: (b, t, 0))
    full = lambda a: pl.BlockSpec(a.shape, lambda b, t: (0,) * a.ndim)
    out_shape = [jax.ShapeDtypeStruct(x.shape, _F32)]
    out_specs = [tile(D_MODEL)]
    if emit_v:
        out_shape.append(jax.ShapeDtypeStruct((bsz, seq, D_SGU), _F32))
        out_specs.append(tile(D_SGU))
    return pl.pallas_call(
        functools.partial(_sgu_kernel, streams=streams, rows=rows),
        grid=grid,
        in_specs=[
            tile(D_MODEL),
            pl.BlockSpec((streams, N_MOD, D_MODEL), lambda b, t: (b, 0, 0)),
            full(g_norm), _layer_spec(w_in, layer, pipeline_mode=pl.Buffered(1)), full(g_v),
            full(w_s), full(mask), full(b_s),
            _layer_spec(w_out, layer, pipeline_mode=pl.Buffered(1)),
        ],
        out_specs=out_specs,
        out_shape=out_shape,
        compiler_params=pltpu.CompilerParams(
            dimension_semantics=("arbitrary", "arbitrary"),
            vmem_limit_bytes=VMEM_LIMIT_BYTES),
        name="sgu_mixer",
    )(x, mod, g_norm, w_in, g_v, w_s, mask, b_s, w_out)


def _ffn_body(x, mod, g, w_up_ref, w_down_ref):
    streams, rows, _ = x.shape
    m = streams * rows
    sh2, sc2, gt2 = mod[:, 3:4], mod[:, 4:5], mod[:, 5:6]
    h = _rms(x, g[2:3] * (1.0 + sc2)) + sh2
    hb = h.reshape(m, D_MODEL).astype(_BF16)
    acc = jnp.zeros((m, D_MODEL), _F32)
    for c in range(D_FF // FF_CHUNK):
        cols = slice(c * FF_CHUNK, (c + 1) * FF_CHUNK)
        a = jnp.maximum(_dot(hb, w_up_ref[:, cols]), 0.0)
        acc = acc + _dot((a * a).astype(_BF16), w_down_ref[cols, :])
    return x + _rms(acc.reshape(streams, rows, D_MODEL), g[3:4] * gt2)


def _ffn_kernel(x_ref, mod_ref, g_ref, w_up_ref, w_down_ref, o_ref):
    o_ref[...] = _ffn_body(x_ref[...], mod_ref[...], g_ref[...], w_up_ref, w_down_ref)


def _ffn_layer(x, mod, g_norm, w_up, w_down, *, layer, streams, rows):
    bsz, seq, _ = x.shape
    grid = (bsz // streams, seq // rows)
    tile = pl.BlockSpec((streams, rows, D_MODEL), lambda b, t: (b, t, 0))
    full = lambda a: pl.BlockSpec(a.shape, lambda b, t: (0,) * a.ndim)
    return pl.pallas_call(
        _ffn_kernel,
        grid=grid,
        in_specs=[
            tile,
            pl.BlockSpec((streams, N_MOD, D_MODEL), lambda b, t: (b, 0, 0)),
            full(g_norm), _layer_spec(w_up, layer, pipeline_mode=pl.Buffered(1)),
            _layer_spec(w_down, layer, pipeline_mode=pl.Buffered(1)),
        ],
        out_specs=tile,
        out_shape=jax.ShapeDtypeStruct(x.shape, _F32),
        compiler_params=pltpu.CompilerParams(
            dimension_semantics=("arbitrary", "arbitrary"),
            vmem_limit_bytes=VMEM_LIMIT_BYTES),
        name="channel_mlp",
    )(x, mod, g_norm, w_up, w_down)


def _window_sums(ext, win):
    tot, span = ext, 1
    while span < win:
        tot = tot + pltpu.roll(tot, span, axis=1)
        span *= 2
    return tot


def _pool_ffn_kernel(*refs, hist, halo_is_h):
    if halo_is_h:
        (x_ref, halo_ref, mod_ref, g_ref, wg_ref, scale_ref, w_up_ref, w_down_ref,
         o_ref, tail_ref) = refs
    else:
        (x_ref, mod_ref, g_ref, wg_ref, scale_ref, w_up_ref, w_down_ref,
         o_ref, tail_ref, carry_ref) = refs
    t = pl.program_id(1)
    x = x_ref[...]
    streams, rows, _ = x.shape
    mod = mod_ref[...]
    sh1, sc1, gt1 = mod[:, 0:1], mod[:, 1:2], mod[:, 2:3]
    g = g_ref[...]

    h = _rms(x, g[0:1] * (1.0 + sc1)) + sh1
    if halo_is_h:
        halo = halo_ref[...]
    else:
        @pl.when(t == 0)
        def _():
            carry_ref[...] = jnp.zeros_like(carry_ref)
        halo = carry_ref[...]
    ext = jnp.concatenate([halo, h], axis=1)
    last = ext[:, rows:, :]
    tail_ref[...] = last
    if not halo_is_h:
        carry_ref[...] = last
    pos = lax.broadcasted_iota(jnp.int32, (1, rows, 1), 1) + (t * rows + hist)
    outs = []
    for grp, win in enumerate(POOL_WINDOWS):
        cols = slice(grp * POOL_GROUP_DIM, (grp + 1) * POOL_GROUP_DIM)
        tot = _window_sums(ext[:, :, cols], win)[:, POOL_HALO:, :]
        inv_cnt = 1.0 / jnp.minimum(win, pos + 1).astype(_F32)
        pooled = (tot * inv_cnt - h[:, :, cols]).reshape(streams * rows, POOL_GROUP_DIM)
        outs.append(_dot(pooled.astype(_BF16), wg_ref[grp]))
    y = (jnp.concatenate(outs, axis=-1) * scale_ref[...]).reshape(streams, rows, D_MODEL)
    x = x + _rms(y, g[1:2] * gt1)
    o_ref[...] = _ffn_body(x, mod, g, w_up_ref, w_down_ref)


def _pool_ffn_layer(x, halo, mod, g_norm, w_grp, scale, w_up, w_down, *, layer, streams,
                    rows, hist):
    bsz, seq, _ = x.shape
    grid = (bsz // streams, seq // rows)
    halo_is_h = halo is not None
    tile = pl.BlockSpec((streams, rows, D_MODEL), lambda b, t: (b, t, 0))
    per_stream = lambda n: pl.BlockSpec((streams, n, D_MODEL), lambda b, t: (b, 0, 0))
    full = lambda a: pl.BlockSpec(a.shape, lambda b, t: (0,) * a.ndim)
    operands = [x] + ([halo] if halo_is_h else []) + [mod, g_norm, w_grp, scale, w_up, w_down]
    in_specs = ([tile] + ([per_stream(POOL_HALO)] if halo_is_h else [])
                + [per_stream(N_MOD), full(g_norm), full(w_grp), full(scale),
                   _layer_spec(w_up, layer, pipeline_mode=pl.Buffered(1)),
                   _layer_spec(w_down, layer, pipeline_mode=pl.Buffered(1))])
    scratch = [] if halo_is_h else [pltpu.VMEM((streams, POOL_HALO, D_MODEL), _F32)]
    return pl.pallas_call(
        functools.partial(_pool_ffn_kernel, hist=hist, halo_is_h=halo_is_h),
        grid=grid,
        in_specs=in_specs,
        out_specs=[tile, per_stream(POOL_HALO)],
        out_shape=[jax.ShapeDtypeStruct(x.shape, _F32),
                   jax.ShapeDtypeStruct((bsz, POOL_HALO, D_MODEL), _F32)],
        scratch_shapes=scratch,
        compiler_params=pltpu.CompilerParams(
            dimension_semantics=("arbitrary", "arbitrary"),
            vmem_limit_bytes=VMEM_LIMIT_BYTES),
        name="pool_mlp",
    )(*operands)


def _sgu_mask(length):
    pos = jnp.arange(SGU_LEN)
    same_chunk = (pos[:, None] // length) == (pos[None, :] // length)
    causal = (pos[None, :] % length) // CHUNK <= (pos[:, None] % length) // CHUNK
    return (same_chunk & causal).astype(_F32)


def _trunk(x, mod, pool_cache, weights, *, streams, rows, ffn_rows):
    (g_norm, sgu_w_in, sgu_g_v, sgu_w_s, sgu_b_s, sgu_w_out, pool_w_grp, pool_scale,
     ffn_w_up, ffn_w_down) = weights
    sample = pool_cache is not None
    seq = x.shape[1]
    length = seq if sample else SGU_LEN
    reps = SGU_LEN // length
    mask = _sgu_mask(length)
    sgu_states, pool_states = [], []
    for i in range(DEPTH):
        j = i // 2
        if i % 2 == 0:
            w_s = jnp.tile(sgu_w_s[j][:, :length, :length], (1, reps, reps))
            b_s = jnp.tile(sgu_b_s[j][:, :length], (1, reps))[:, :, None]
            outs = _sgu_layer(x, mod[i], g_norm[i], sgu_w_in, sgu_g_v[j][None, :],
                              w_s, mask, b_s, sgu_w_out, layer=j, streams=streams,
                              rows=ffn_rows, emit_v=sample)
            x = outs[0]
            if sample:
                sgu_states.append(outs[1])
            x = _ffn_layer(x, mod[i], g_norm[i], ffn_w_up, ffn_w_down, layer=i,
                           streams=streams, rows=ffn_rows)
        else:
            halo = jnp.pad(pool_cache[j], ((0, 0), (1, 0), (0, 0))) if sample else None
            x, tail = _pool_ffn_layer(
                x, halo, mod[i], g_norm[i], pool_w_grp[j], pool_scale[j][None, :],
                ffn_w_up, ffn_w_down, layer=i, streams=streams, rows=ffn_rows,
                hist=POOL_STATE if sample else 0)
            pool_states.append(tail[:, 1:, :])
    return x, sgu_states, pool_states


def kernel(x_prompt, x_sample, cache_pool, c_prompt, c_sample, g_norm, w_ada, b_ada,
           sgu_w_in, sgu_g_v, sgu_w_s, sgu_b_s, sgu_w_out, pool_w_grp, pool_scale,
           ffn_w_up, ffn_w_down):
    n_p, n_s = c_prompt.shape[0], c_sample.shape[0]
    c_all = jnp.concatenate(
        [c_prompt, c_sample, jnp.zeros((MOD_ROWS - n_p - n_s, D_MODEL), _F32)], axis=0)
    mod = _modulation(c_all, w_ada, b_ada)
    mod = mod.reshape(DEPTH, MOD_ROWS, N_MOD, D_MODEL)
    mod_p, mod_s = mod[:, :n_p], mod[:, n_p:n_p + n_s]

    weights = (g_norm, sgu_w_in.astype(_BF16), sgu_g_v, sgu_w_s, sgu_b_s,
               sgu_w_out.astype(_BF16), pool_w_grp.astype(_BF16), pool_scale,
               ffn_w_up.astype(_BF16), ffn_w_down.astype(_BF16))

    y_p, _, pool_p = _trunk(x_prompt, mod_p, None, weights,
                            streams=1, rows=TILE_TOKENS, ffn_rows=FFN_TILE_TOKENS)
    s_rows = x_sample.shape[1]
    y_s, sgu_s, pool_s = _trunk(x_sample, mod_s, cache_pool, weights,
                                streams=TILE_TOKENS // s_rows, rows=s_rows, ffn_rows=s_rows)
    return (y_p, y_s, jnp.stack(pool_p, axis=0), jnp.stack(sgu_s, axis=0),
            jnp.stack(pool_s, axis=0))
```

```python
import functools

import jax
import jax.numpy as jnp
from jax import lax
from jax.experimental import pallas as pl
from jax.experimental.pallas import tpu as pltpu

D_MODEL = 1024
DEPTH = 4
N_MOD = 6
EPS = 1e-6
D_SGU = 2 * D_MODEL
SGU_GROUPS = 8
SGU_GROUP_DIM = D_SGU // SGU_GROUPS
SGU_LEN = 128
SGU_PAIR = 2
CHUNK = 64
POOL_WINDOWS = (2, 4, 8, 16)
POOL_GROUP_DIM = D_MODEL // len(POOL_WINDOWS)
POOL_HALO = max(POOL_WINDOWS)
POOL_STATE = POOL_HALO - 1
D_FF = 4 * D_MODEL

SGU_TILE_TOKENS = 512
MLP_TILE_TOKENS = 1024
FF_CHUNK = 512
MOD_COLS = 1536
MOD_ROWS = 32
VMEM_LIMIT_BYTES = 56 * 1024 * 1024

_F32 = jnp.float32
_BF16 = jnp.bfloat16


def _dot(a, b):
    return jnp.dot(a, b, preferred_element_type=_F32)


def _rms(x, gain, gain_first=True):
    inv = lax.rsqrt(jnp.mean(x * x, axis=-1, keepdims=True) + EPS)
    return (x * gain) * inv if gain_first else x * inv * gain


def _layer_spec(stacked, layer):
    index = (layer,) + (0,) * (stacked.ndim - 1)
    return pl.BlockSpec((None,) + stacked.shape[1:], lambda *_: index,
                        pipeline_mode=pl.Buffered(1))


def _gelu_tanh(x):
    c = 0.7978845608028654
    half = 0.5 * x
    return half + half * jnp.tanh(x * (c + (c * 0.044715) * (x * x)))


def _mod_kernel(c_ref, w_ref, b_ref, o_ref):
    c = c_ref[...]
    s = (c * jax.nn.sigmoid(c)).astype(_BF16)
    o_ref[0] = _dot(s, w_ref[0].astype(_BF16)) + b_ref[0]


def _modulation(c_all, w_ada, b_ada):
    n_col = N_MOD * D_MODEL
    return pl.pallas_call(
        _mod_kernel,
        grid=(DEPTH, n_col // MOD_COLS),
        in_specs=[
            pl.BlockSpec((MOD_ROWS, D_MODEL), lambda i, j: (0, 0)),
            pl.BlockSpec((1, D_MODEL, MOD_COLS), lambda i, j: (i, 0, j)),
            pl.BlockSpec((1, 1, MOD_COLS), lambda i, j: (i, 0, j)),
        ],
        out_specs=pl.BlockSpec((1, MOD_ROWS, MOD_COLS), lambda i, j: (i, 0, j)),
        out_shape=jax.ShapeDtypeStruct((DEPTH, MOD_ROWS, n_col), _F32),
        compiler_params=pltpu.CompilerParams(
            dimension_semantics=("arbitrary", "arbitrary"),
            vmem_limit_bytes=VMEM_LIMIT_BYTES),
        name="adaln_mod",
    )(c_all, w_ada, b_ada.reshape(DEPTH, 1, n_col))


def _sgu_kernel(x_ref, mod_ref, g_ref, w_in_ref, gv_ref, ws_ref, mask_ref, bs_ref,
                w_out_ref, o_ref, *v_ref, streams, rows):
    m = streams * rows
    x = x_ref[...]
    mod = mod_ref[...]
    sh1, sc1, gt1 = mod[:, 0:1], mod[:, 1:2], mod[:, 2:3]
    g = g_ref[...]
    h = _rms(x, g[0:1], gain_first=False) * (1.0 + sc1) + sh1
    hb = h.reshape(m, D_MODEL).astype(_BF16)

    n_pairs = SGU_GROUPS // SGU_PAIR
    pair_cols = [slice(p * SGU_PAIR * SGU_GROUP_DIM, (p + 1) * SGU_PAIR * SGU_GROUP_DIM)
                 for p in range(n_pairs)]
    v_parts, u_parts = [], []
    for cols in pair_cols:
        vcols = slice(D_SGU + cols.start, D_SGU + cols.stop)
        v_parts.append(_gelu_tanh(_dot(hb, w_in_ref[:, vcols])))
        u_parts.append(_gelu_tanh(_dot(hb, w_in_ref[:, cols])))
    v = _rms(jnp.concatenate(v_parts, axis=1), gv_ref[...], gain_first=False)
    if v_ref:
        v_ref[0][...] = v.reshape(streams, rows, D_SGU)
    vb = v.astype(_BF16)

    mask = mask_ref[...]
    n_chunks = m // SGU_LEN
    acc = jnp.zeros((m, D_MODEL), _F32)
    for pair in range(n_pairs):
        pcols = pair_cols[pair]
        u = u_parts[pair]
        s_pair = []
        for grp in range(pair * SGU_PAIR, (pair + 1) * SGU_PAIR):
            cols = slice(grp * SGU_GROUP_DIM, (grp + 1) * SGU_GROUP_DIM)
            w = (ws_ref[grp] * mask).astype(_BF16)
            v_wide = jnp.concatenate(
                [vb[n * SGU_LEN:(n + 1) * SGU_LEN, cols] for n in range(n_chunks)], axis=1)
            s_wide = _dot(w, v_wide) + bs_ref[grp]
            s_pair.append(jnp.concatenate(
                [s_wide[:, n * SGU_GROUP_DIM:(n + 1) * SGU_GROUP_DIM]
                 for n in range(n_chunks)], axis=0))
        s = jnp.concatenate(s_pair, axis=1)
        acc = acc + _dot((u * s).astype(_BF16), w_out_ref[pcols, :])

    mix = _rms(acc, g[1:2], gain_first=False).reshape(streams, rows, D_MODEL)
    o_ref[...] = x + gt1 * mix


def _sgu_layer(x, mod, g_norm, w_in, g_v, w_s, mask, b_s, w_out, *, layer, streams, rows,
               emit_v):
    bsz, seq, _ = x.shape
    grid = (bsz // streams, seq // rows)
    tile = lambda d: pl.BlockSpec((streams, rows, d), lambda b, t: (b, t, 0))
    full = lambda a: pl.BlockSpec(a.shape, lambda b, t: (0,) * a.ndim)
    out_shape = [jax.ShapeDtypeStruct(x.shape, _F32)]
    out_specs = [tile(D_MODEL)]
    if emit_v:
        out_shape.append(jax.ShapeDtypeStruct((bsz, seq, D_SGU), _F32))
        out_specs.append(tile(D_SGU))
    return pl.pallas_call(
        functools.partial(_sgu_kernel, streams=streams, rows=rows),
        grid=grid,
        in_specs=[
            tile(D_MODEL),
            pl.BlockSpec((streams, N_MOD, D_MODEL), lambda b, t: (b, 0, 0)),
            full(g_norm), _layer_spec(w_in, layer), full(g_v), full(w_s), full(mask),
            full(b_s), _layer_spec(w_out, layer),
        ],
        out_specs=out_specs,
        out_shape=out_shape,
        compiler_params=pltpu.CompilerParams(
            dimension_semantics=("arbitrary", "arbitrary"),
            vmem_limit_bytes=VMEM_LIMIT_BYTES),
        name="sgu_mixer",
    )(x, mod, g_norm, w_in, g_v, w_s, mask, b_s, w_out)


def _ffn_body(x, mod, g, w_up_ref, w_down_ref):
    streams, rows, _ = x.shape
    m = streams * rows
    sh2, sc2, gt2 = mod[:, 3:4], mod[:, 4:5], mod[:, 5:6]
    h = _rms(x, g[2:3] * (1.0 + sc2)) + sh2
    hb = h.reshape(m, D_MODEL).astype(_BF16)
    acc = jnp.zeros((m, D_MODEL), _F32)
    for c in range(D_FF // FF_CHUNK):
        cols = slice(c * FF_CHUNK, (c + 1) * FF_CHUNK)
        a = jnp.maximum(_dot(hb, w_up_ref[:, cols]), 0.0)
        acc = acc + _dot((a * a).astype(_BF16), w_down_ref[cols, :])
    return x + _rms(acc.reshape(streams, rows, D_MODEL), g[3:4] * gt2)


def _ffn_kernel(x_ref, mod_ref, g_ref, w_up_ref, w_down_ref, o_ref):
    o_ref[...] = _ffn_body(x_ref[...], mod_ref[...], g_ref[...], w_up_ref, w_down_ref)


def _ffn_layer(x, mod, g_norm, w_up, w_down, *, layer, streams, rows):
    bsz, seq, _ = x.shape
    grid = (bsz // streams, seq // rows)
    tile = pl.BlockSpec((streams, rows, D_MODEL), lambda b, t: (b, t, 0))
    full = lambda a: pl.BlockSpec(a.shape, lambda b, t: (0,) * a.ndim)
    return pl.pallas_call(
        _ffn_kernel,
        grid=grid,
        in_specs=[
            tile,
            pl.BlockSpec((streams, N_MOD, D_MODEL), lambda b, t: (b, 0, 0)),
            full(g_norm), _layer_spec(w_up, layer), _layer_spec(w_down, layer),
        ],
        out_specs=tile,
        out_shape=jax.ShapeDtypeStruct(x.shape, _F32),
        compiler_params=pltpu.CompilerParams(
            dimension_semantics=("arbitrary", "arbitrary"),
            vmem_limit_bytes=VMEM_LIMIT_BYTES),
        name="channel_mlp",
    )(x, mod, g_norm, w_up, w_down)


def _window_sums(ext, win):
    tot, span = ext, 1
    while span < win:
        tot = tot + pltpu.roll(tot, span, axis=1)
        span *= 2
    return tot


def _pool_ffn_kernel(*refs, hist, halo_is_h):
    if halo_is_h:
        (x_ref, halo_ref, mod_ref, g_ref, wg_ref, scale_ref, w_up_ref, w_down_ref,
         o_ref, tail_ref) = refs
    else:
        (x_ref, mod_ref, g_ref, wg_ref, scale_ref, w_up_ref, w_down_ref,
         o_ref, tail_ref, carry_ref) = refs
    t = pl.program_id(1)
    x = x_ref[...]
    streams, rows, _ = x.shape
    mod = mod_ref[...]
    sh1, sc1, gt1 = mod[:, 0:1], mod[:, 1:2], mod[:, 2:3]
    g = g_ref[...]

    h = _rms(x, g[0:1] * (1.0 + sc1)) + sh1
    if halo_is_h:
        halo = halo_ref[...]
    else:
        @pl.when(t == 0)
        def _():
            carry_ref[...] = jnp.zeros_like(carry_ref)
        halo = carry_ref[...]
    ext = jnp.concatenate([halo, h], axis=1)
    last = ext[:, rows:, :]
    tail_ref[...] = last
    if not halo_is_h:
        carry_ref[...] = last
    pos = lax.broadcasted_iota(jnp.int32, (1, rows, 1), 1) + (t * rows + hist)
    outs = []
    for grp, win in enumerate(POOL_WINDOWS):
        cols = slice(grp * POOL_GROUP_DIM, (grp + 1) * POOL_GROUP_DIM)
        tot = _window_sums(ext[:, :, cols], win)[:, POOL_HALO:, :]
        inv_cnt = 1.0 / jnp.minimum(win, pos + 1).astype(_F32)
        pooled = (tot * inv_cnt - h[:, :, cols]).reshape(streams * rows, POOL_GROUP_DIM)
        outs.append(_dot(pooled.astype(_BF16), wg_ref[grp]))
    y = (jnp.concatenate(outs, axis=-1) * scale_ref[...]).reshape(streams, rows, D_MODEL)
    x = x + _rms(y, g[1:2] * gt1)
    o_ref[...] = _ffn_body(x, mod, g, w_up_ref, w_down_ref)


def _pool_ffn_layer(x, halo, mod, g_norm, w_grp, scale, w_up, w_down, *, layer, streams,
                    rows, hist):
    bsz, seq, _ = x.shape
    grid = (bsz // streams, seq // rows)
    halo_is_h = halo is not None
    tile = pl.BlockSpec((streams, rows, D_MODEL), lambda b, t: (b, t, 0))
    per_stream = lambda n: pl.BlockSpec((streams, n, D_MODEL), lambda b, t: (b, 0, 0))
    full = lambda a: pl.BlockSpec(a.shape, lambda b, t: (0,) * a.ndim)
    operands = [x] + ([halo] if halo_is_h else []) + [mod, g_norm, w_grp, scale, w_up, w_down]
    in_specs = ([tile] + ([per_stream(POOL_HALO)] if halo_is_h else [])
                + [per_stream(N_MOD), full(g_norm), full(w_grp), full(scale),
                   _layer_spec(w_up, layer), _layer_spec(w_down, layer)])
    scratch = [] if halo_is_h else [pltpu.VMEM((streams, POOL_HALO, D_MODEL), _F32)]
    return pl.pallas_call(
        functools.partial(_pool_ffn_kernel, hist=hist, halo_is_h=halo_is_h),
        grid=grid,
        in_specs=in_specs,
        out_specs=[tile, per_stream(POOL_HALO)],
        out_shape=[jax.ShapeDtypeStruct(x.shape, _F32),
                   jax.ShapeDtypeStruct((bsz, POOL_HALO, D_MODEL), _F32)],
        scratch_shapes=scratch,
        compiler_params=pltpu.CompilerParams(
            dimension_semantics=("arbitrary", "arbitrary"),
            vmem_limit_bytes=VMEM_LIMIT_BYTES),
        name="pool_mlp",
    )(*operands)


def _sgu_mask(length):
    pos = jnp.arange(SGU_LEN)
    same_chunk = (pos[:, None] // length) == (pos[None, :] // length)
    causal = (pos[None, :] % length) // CHUNK <= (pos[:, None] % length) // CHUNK
    return (same_chunk & causal).astype(_F32)


def _trunk(x, mod, pool_cache, weights, *, streams, sgu_rows, mlp_rows):
    (g_norm, sgu_w_in, sgu_g_v, sgu_w_s, sgu_b_s, sgu_w_out, pool_w_grp, pool_scale,
     ffn_w_up, ffn_w_down) = weights
    sample = pool_cache is not None
    seq = x.shape[1]
    length = seq if sample else SGU_LEN
    reps = SGU_LEN // length
    mask = _sgu_mask(length)
    sgu_states, pool_states = [], []
    for i in range(DEPTH):
        j = i // 2
        if i % 2 == 0:
            w_s = jnp.tile(sgu_w_s[j][:, :length, :length], (1, reps, reps))
            b_s = jnp.tile(sgu_b_s[j][:, :length], (1, reps))[:, :, None]
            outs = _sgu_layer(x, mod[i], g_norm[i], sgu_w_in, sgu_g_v[j][None, :],
                              w_s, mask, b_s, sgu_w_out, layer=j, streams=streams,
                              rows=sgu_rows, emit_v=sample)
            x = outs[0]
            if sample:
                sgu_states.append(outs[1])
            x = _ffn_layer(x, mod[i], g_norm[i], ffn_w_up, ffn_w_down, layer=i,
                           streams=streams, rows=mlp_rows)
        else:
            halo = jnp.pad(pool_cache[j], ((0, 0), (1, 0), (0, 0))) if sample else None
            x, tail = _pool_ffn_layer(
                x, halo, mod[i], g_norm[i], pool_w_grp[j], pool_scale[j][None, :],
                ffn_w_up, ffn_w_down, layer=i, streams=streams, rows=mlp_rows,
                hist=POOL_STATE if sample else 0)
            pool_states.append(tail[:, 1:, :])
    return x, sgu_states, pool_states


def kernel(x_prompt, x_sample, cache_pool, c_prompt, c_sample, g_norm, w_ada, b_ada,
           sgu_w_in, sgu_g_v, sgu_w_s, sgu_b_s, sgu_w_out, pool_w_grp, pool_scale,
           ffn_w_up, ffn_w_down):
    n_p, n_s = c_prompt.shape[0], c_sample.shape[0]
    c_all = jnp.concatenate(
        [c_prompt, c_sample, jnp.zeros((MOD_ROWS - n_p - n_s, D_MODEL), _F32)], axis=0)
    mod = _modulation(c_all, w_ada, b_ada)
    mod = mod.reshape(DEPTH, MOD_ROWS, N_MOD, D_MODEL)
    mod_p, mod_s = mod[:, :n_p], mod[:, n_p:n_p + n_s]

    weights = (g_norm, sgu_w_in.astype(_BF16), sgu_g_v, sgu_w_s, sgu_b_s,
               sgu_w_out.astype(_BF16), pool_w_grp.astype(_BF16), pool_scale,
               ffn_w_up.astype(_BF16), ffn_w_down.astype(_BF16))

    y_p, _, pool_p = _trunk(x_prompt, mod_p, None, weights,
                            streams=1, sgu_rows=SGU_TILE_TOKENS, mlp_rows=MLP_TILE_TOKENS)
    n_streams, s_rows = x_sample.shape[:2]
    y_s, sgu_s, pool_s = _trunk(x_sample, mod_s, cache_pool, weights,
                                streams=n_streams, sgu_rows=s_rows, mlp_rows=s_rows)
    return (y_p, y_s, jnp.stack(pool_p, axis=0), jnp.stack(sgu_s, axis=0),
            jnp.stack(pool_s, axis=0))
```

```python
import functools

import jax
import jax.numpy as jnp
from jax import lax
from jax.experimental import pallas as pl
from jax.experimental.pallas import tpu as pltpu

D_MODEL = 1024
DEPTH = 4
N_MOD = 6
EPS = 1e-6
D_SGU = 2 * D_MODEL
SGU_GROUPS = 8
SGU_GROUP_DIM = D_SGU // SGU_GROUPS
SGU_LEN = 128
SGU_PAIR = 2
CHUNK = 64
POOL_WINDOWS = (2, 4, 8, 16)
POOL_GROUP_DIM = D_MODEL // len(POOL_WINDOWS)
POOL_HALO = max(POOL_WINDOWS)
POOL_STATE = POOL_HALO - 1
D_FF = 4 * D_MODEL

SGU_TILE_TOKENS = 512
MLP_TILE_TOKENS = 1024
FF_CHUNK = 512
POOL_SLICES = 4
POOL_EARLY_CHUNKS = 4
MOD_COLS = 1536
MOD_ROWS = 32
VMEM_LIMIT_BYTES = 56 * 1024 * 1024

_F32 = jnp.float32
_BF16 = jnp.bfloat16


def _dot(a, b):
    return jnp.dot(a, b, preferred_element_type=_F32)


def _rms(x, gain, gain_first=True):
    inv = lax.rsqrt(jnp.mean(x * x, axis=-1, keepdims=True) + EPS)
    return (x * gain) * inv if gain_first else x * inv * gain


def _layer_spec(stacked, layer):
    index = (layer,) + (0,) * (stacked.ndim - 1)
    return pl.BlockSpec((None,) + stacked.shape[1:], lambda *_: index,
                        pipeline_mode=pl.Buffered(1))


def _gelu_tanh(x):
    c = 0.7978845608028654
    half = 0.5 * x
    return half + half * jnp.tanh(x * (c + (c * 0.044715) * (x * x)))


def _mod_kernel(c_ref, w_ref, b_ref, o_ref):
    c = c_ref[...]
    s = (c * jax.nn.sigmoid(c)).astype(_BF16)
    o_ref[0] = _dot(s, w_ref[0].astype(_BF16)) + b_ref[0]


def _modulation(c_all, w_ada, b_ada):
    n_col = N_MOD * D_MODEL
    return pl.pallas_call(
        _mod_kernel,
        grid=(DEPTH, n_col // MOD_COLS),
        in_specs=[
            pl.BlockSpec((MOD_ROWS, D_MODEL), lambda i, j: (0, 0)),
            pl.BlockSpec((1, D_MODEL, MOD_COLS), lambda i, j: (i, 0, j)),
            pl.BlockSpec((1, 1, MOD_COLS), lambda i, j: (i, 0, j)),
        ],
        out_specs=pl.BlockSpec((1, MOD_ROWS, MOD_COLS), lambda i, j: (i, 0, j)),
        out_shape=jax.ShapeDtypeStruct((DEPTH, MOD_ROWS, n_col), _F32),
        compiler_params=pltpu.CompilerParams(
            dimension_semantics=("arbitrary", "arbitrary"),
            vmem_limit_bytes=VMEM_LIMIT_BYTES),
        name="adaln_mod",
    )(c_all, w_ada, b_ada.reshape(DEPTH, 1, n_col))


def _sgu_kernel(x_ref, mod_ref, g_ref, w_in_ref, gv_ref, ws_ref, mask_ref, bs_ref,
                w_out_ref, o_ref, *v_ref, streams, rows):
    m = streams * rows
    x = x_ref[...]
    mod = mod_ref[...]
    sh1, sc1, gt1 = mod[:, 0:1], mod[:, 1:2], mod[:, 2:3]
    g = g_ref[...]
    h = _rms(x, g[0:1], gain_first=False) * (1.0 + sc1) + sh1
    hb = h.reshape(m, D_MODEL).astype(_BF16)

    n_pairs = SGU_GROUPS // SGU_PAIR
    pair_cols = [slice(p * SGU_PAIR * SGU_GROUP_DIM, (p + 1) * SGU_PAIR * SGU_GROUP_DIM)
                 for p in range(n_pairs)]
    v_parts, u_parts = [], []
    for cols in pair_cols:
        vcols = slice(D_SGU + cols.start, D_SGU + cols.stop)
        v_parts.append(_gelu_tanh(_dot(hb, w_in_ref[:, vcols])))
        u_parts.append(_gelu_tanh(_dot(hb, w_in_ref[:, cols])))
    v = _rms(jnp.concatenate(v_parts, axis=1), gv_ref[...], gain_first=False)
    if v_ref:
        v_ref[0][...] = v.reshape(streams, rows, D_SGU)
    vb = v.astype(_BF16)

    mask = mask_ref[...]
    n_chunks = m // SGU_LEN
    acc = jnp.zeros((m, D_MODEL), _F32)
    for pair in range(n_pairs):
        pcols = pair_cols[pair]
        u = u_parts[pair]
        s_pair = []
        for grp in range(pair * SGU_PAIR, (pair + 1) * SGU_PAIR):
            cols = slice(grp * SGU_GROUP_DIM, (grp + 1) * SGU_GROUP_DIM)
            w = (ws_ref[grp] * mask).astype(_BF16)
            v_wide = jnp.concatenate(
                [vb[n * SGU_LEN:(n + 1) * SGU_LEN, cols] for n in range(n_chunks)], axis=1)
            s_wide = _dot(w, v_wide) + bs_ref[grp]
            s_pair.append(jnp.concatenate(
                [s_wide[:, n * SGU_GROUP_DIM:(n + 1) * SGU_GROUP_DIM]
                 for n in range(n_chunks)], axis=0))
        s = jnp.concatenate(s_pair, axis=1)
        acc = acc + _dot((u * s).astype(_BF16), w_out_ref[pcols, :])

    mix = _rms(acc, g[1:2], gain_first=False).reshape(streams, rows, D_MODEL)
    o_ref[...] = x + gt1 * mix


def _sgu_layer(x, mod, g_norm, w_in, g_v, w_s, mask, b_s, w_out, *, layer, streams, rows,
               emit_v):
    bsz, seq, _ = x.shape
    grid = (bsz // streams, seq // rows)
    tile = lambda d: pl.BlockSpec((streams, rows, d), lambda b, t: (b, t, 0))
    full = lambda a: pl.BlockSpec(a.shape, lambda b, t: (0,) * a.ndim)
    out_shape = [jax.ShapeDtypeStruct(x.shape, _F32)]
    out_specs = [tile(D_MODEL)]
    if emit_v:
        out_shape.append(jax.ShapeDtypeStruct((bsz, seq, D_SGU), _F32))
        out_specs.append(tile(D_SGU))
    return pl.pallas_call(
        functools.partial(_sgu_kernel, streams=streams, rows=rows),
        grid=grid,
        in_specs=[
            tile(D_MODEL),
            pl.BlockSpec((streams, N_MOD, D_MODEL), lambda b, t: (b, 0, 0)),
            full(g_norm), _layer_spec(w_in, layer), full(g_v), full(w_s), full(mask),
            full(b_s), _layer_spec(w_out, layer),
        ],
        out_specs=out_specs,
        out_shape=out_shape,
        compiler_params=pltpu.CompilerParams(
            dimension_semantics=("arbitrary", "arbitrary"),
            vmem_limit_bytes=VMEM_LIMIT_BYTES),
        name="sgu_mixer",
    )(x, mod, g_norm, w_in, g_v, w_s, mask, b_s, w_out)


def _ffn_pre(x, mod, g):
    streams, rows, _ = x.shape
    sh2, sc2 = mod[:, 3:4], mod[:, 4:5]
    h = _rms(x, g[2:3] * (1.0 + sc2)) + sh2
    return h.reshape(streams * rows, D_MODEL).astype(_BF16)


def _ffn_up(hb, w_up_ref, c):
    a = jnp.maximum(_dot(hb, w_up_ref[:, c * FF_CHUNK:(c + 1) * FF_CHUNK]), 0.0)
    return (a * a).astype(_BF16)


def _ffn_post(x, mod, g, hb, early, w_up_ref, w_down_ref):
    streams, rows, _ = x.shape
    acc = jnp.zeros((streams * rows, D_MODEL), _F32)
    for c in range(D_FF // FF_CHUNK):
        a = early[c] if c < len(early) else _ffn_up(hb, w_up_ref, c)
        acc = acc + _dot(a, w_down_ref[c * FF_CHUNK:(c + 1) * FF_CHUNK, :])
    return x + _rms(acc.reshape(streams, rows, D_MODEL), g[3:4] * mod[:, 5:6])


def _ffn_body(x, mod, g, w_up_ref, w_down_ref):
    return _ffn_post(x, mod, g, _ffn_pre(x, mod, g), [], w_up_ref, w_down_ref)


def _ffn_kernel(x_ref, mod_ref, g_ref, w_up_ref, w_down_ref, o_ref):
    o_ref[...] = _ffn_body(x_ref[...], mod_ref[...], g_ref[...], w_up_ref, w_down_ref)


def _ffn_layer(x, mod, g_norm, w_up, w_down, *, layer, streams, rows):
    bsz, seq, _ = x.shape
    grid = (bsz // streams, seq // rows)
    tile = pl.BlockSpec((streams, rows, D_MODEL), lambda b, t: (b, t, 0))
    full = lambda a: pl.BlockSpec(a.shape, lambda b, t: (0,) * a.ndim)
    return pl.pallas_call(
        _ffn_kernel,
        grid=grid,
        in_specs=[
            tile,
            pl.BlockSpec((streams, N_MOD, D_MODEL), lambda b, t: (b, 0, 0)),
            full(g_norm), _layer_spec(w_up, layer), _layer_spec(w_down, layer),
        ],
        out_specs=tile,
        out_shape=jax.ShapeDtypeStruct(x.shape, _F32),
        compiler_params=pltpu.CompilerParams(
            dimension_semantics=("arbitrary", "arbitrary"),
            vmem_limit_bytes=VMEM_LIMIT_BYTES),
        name="channel_mlp",
    )(x, mod, g_norm, w_up, w_down)


def _window_sums(ext, win):
    tot, span = ext, 1
    while span < win:
        tot = tot + pltpu.roll(tot, span, axis=1)
        span *= 2
    return tot


def _pool_ffn_kernel(*refs, hist, halo_is_h):
    if halo_is_h:
        (x_ref, halo_ref, mod_ref, g_ref, wg_ref, scale_ref, w_up_ref, w_down_ref,
         o_ref, tail_ref) = refs
    else:
        (x_ref, mod_ref, g_ref, wg_ref, scale_ref, w_up_ref, w_down_ref,
         o_ref, tail_ref, carry_ref) = refs
    t = pl.program_id(1)
    streams, rows, _ = x_ref.shape
    mod = mod_ref[...]
    sh1, sc1, gt1 = mod[:, 0:1], mod[:, 1:2], mod[:, 2:3]
    g = g_ref[...]
    if halo_is_h:
        halo = halo_ref[...]
    else:
        @pl.when(t == 0)
        def _():
            carry_ref[...] = jnp.zeros_like(carry_ref)
        halo = carry_ref[...]

    n_slices = POOL_SLICES if streams == 1 and rows % (POOL_SLICES * POOL_HALO) == 0 else 1
    n_early = POOL_EARLY_CHUNKS if n_slices > 1 else 0
    sl_rows = rows // n_slices
    x1_parts, hb_parts, early = [], [], [[] for _ in range(n_early)]
    for r in range(n_slices):
        x = x_ref[:, r * sl_rows:(r + 1) * sl_rows, :]
        h = _rms(x, g[0:1] * (1.0 + sc1)) + sh1
        ext = jnp.concatenate([halo, h], axis=1)
        halo = ext[:, sl_rows:, :]
        pos = (lax.broadcasted_iota(jnp.int32, (1, sl_rows, 1), 1)
               + (t * rows + r * sl_rows + hist))
        outs = []
        for grp, win in enumerate(POOL_WINDOWS):
            cols = slice(grp * POOL_GROUP_DIM, (grp + 1) * POOL_GROUP_DIM)
            tot = _window_sums(ext[:, :, cols], win)[:, POOL_HALO:, :]
            inv_cnt = 1.0 / jnp.minimum(win, pos + 1).astype(_F32)
            pooled = (tot * inv_cnt - h[:, :, cols]).reshape(streams * sl_rows, POOL_GROUP_DIM)
            outs.append(_dot(pooled.astype(_BF16), wg_ref[grp]))
        y = jnp.concatenate(outs, axis=-1) * scale_ref[...]
        x1 = x + _rms(y.reshape(streams, sl_rows, D_MODEL), g[1:2] * gt1)
        hb = _ffn_pre(x1, mod, g)
        if r > 0:
            for c in range(n_early):
                early[c].append(_ffn_up(hb_parts[r - 1], w_up_ref, c))
        x1_parts.append(x1)
        hb_parts.append(hb)
    for c in range(n_early):
        early[c].append(_ffn_up(hb_parts[-1], w_up_ref, c))
    tail_ref[...] = halo
    if not halo_is_h:
        carry_ref[...] = halo
    x1 = jnp.concatenate(x1_parts, axis=1)
    hb = jnp.concatenate(hb_parts, axis=0)
    early = [jnp.concatenate(parts, axis=0) for parts in early]
    o_ref[...] = _ffn_post(x1, mod, g, hb, early, w_up_ref, w_down_ref)


def _pool_ffn_layer(x, halo, mod, g_norm, w_grp, scale, w_up, w_down, *, layer, streams,
                    rows, hist):
    bsz, seq, _ = x.shape
    grid = (bsz // streams, seq // rows)
    halo_is_h = halo is not None
    tile = pl.BlockSpec((streams, rows, D_MODEL), lambda b, t: (b, t, 0))
    per_stream = lambda n: pl.BlockSpec((streams, n, D_MODEL), lambda b, t: (b, 0, 0))
    full = lambda a: pl.BlockSpec(a.shape, lambda b, t: (0,) * a.ndim)
    operands = [x] + ([halo] if halo_is_h else []) + [mod, g_norm, w_grp, scale, w_up, w_down]
    in_specs = ([tile] + ([per_stream(POOL_HALO)] if halo_is_h else [])
                + [per_stream(N_MOD), full(g_norm), full(w_grp), full(scale),
                   _layer_spec(w_up, layer), _layer_spec(w_down, layer)])
    scratch = [] if halo_is_h else [pltpu.VMEM((streams, POOL_HALO, D_MODEL), _F32)]
    return pl.pallas_call(
        functools.partial(_pool_ffn_kernel, hist=hist, halo_is_h=halo_is_h),
        grid=grid,
        in_specs=in_specs,
        out_specs=[tile, per_stream(POOL_HALO)],
        out_shape=[jax.ShapeDtypeStruct(x.shape, _F32),
                   jax.ShapeDtypeStruct((bsz, POOL_HALO, D_MODEL), _F32)],
        scratch_shapes=scratch,
        compiler_params=pltpu.CompilerParams(
            dimension_semantics=("arbitrary", "arbitrary"),
            vmem_limit_bytes=VMEM_LIMIT_BYTES),
        name="pool_mlp",
    )(*operands)


def _sgu_mask(length):
    pos = jnp.arange(SGU_LEN)
    same_chunk = (pos[:, None] // length) == (pos[None, :] // length)
    causal = (pos[None, :] % length) // CHUNK <= (pos[:, None] % length) // CHUNK
    return (same_chunk & causal).astype(_F32)


def _trunk(x, mod, pool_cache, weights, *, streams, sgu_rows, mlp_rows):
    (g_norm, sgu_w_in, sgu_g_v, sgu_w_s, sgu_b_s, sgu_w_out, pool_w_grp, pool_scale,
     ffn_w_up, ffn_w_down) = weights
    sample = pool_cache is not None
    seq = x.shape[1]
    length = seq if sample else SGU_LEN
    reps = SGU_LEN // length
    mask = _sgu_mask(length)
    sgu_states, pool_states = [], []
    for i in range(DEPTH):
        j = i // 2
        if i % 2 == 0:
            w_s = jnp.tile(sgu_w_s[j][:, :length, :length], (1, reps, reps))
            b_s = jnp.tile(sgu_b_s[j][:, :length], (1, reps))[:, :, None]
            outs = _sgu_layer(x, mod[i], g_norm[i], sgu_w_in, sgu_g_v[j][None, :],
                              w_s, mask, b_s, sgu_w_out, layer=j, streams=streams,
                              rows=sgu_rows, emit_v=sample)
            x = outs[0]
            if sample:
                sgu_states.append(outs[1])
            x = _ffn_layer(x, mod[i], g_norm[i], ffn_w_up, ffn_w_down, layer=i,
                           streams=streams, rows=mlp_rows)
        else:
            halo = jnp.pad(pool_cache[j], ((0, 0), (1, 0), (0, 0))) if sample else None
            x, tail = _pool_ffn_layer(
                x, halo, mod[i], g_norm[i], pool_w_grp[j], pool_scale[j][None, :],
                ffn_w_up, ffn_w_down, layer=i, streams=streams, rows=mlp_rows,
                hist=POOL_STATE if sample else 0)
            pool_states.append(tail[:, 1:, :])
    return x, sgu_states, pool_states


def kernel(x_prompt, x_sample, cache_pool, c_prompt, c_sample, g_norm, w_ada, b_ada,
           sgu_w_in, sgu_g_v, sgu_w_s, sgu_b_s, sgu_w_out, pool_w_grp, pool_scale,
           ffn_w_up, ffn_w_down):
    n_p, n_s = c_prompt.shape[0], c_sample.shape[0]
    c_all = jnp.concatenate(
        [c_prompt, c_sample, jnp.zeros((MOD_ROWS - n_p - n_s, D_MODEL), _F32)], axis=0)
    mod = _modulation(c_all, w_ada, b_ada)
    mod = mod.reshape(DEPTH, MOD_ROWS, N_MOD, D_MODEL)
    mod_p, mod_s = mod[:, :n_p], mod[:, n_p:n_p + n_s]

    weights = (g_norm, sgu_w_in.astype(_BF16), sgu_g_v, sgu_w_s, sgu_b_s,
               sgu_w_out.astype(_BF16), pool_w_grp.astype(_BF16), pool_scale,
               ffn_w_up.astype(_BF16), ffn_w_down.astype(_BF16))

    y_p, _, pool_p = _trunk(x_prompt, mod_p, None, weights,
                            streams=1, sgu_rows=SGU_TILE_TOKENS, mlp_rows=MLP_TILE_TOKENS)
    n_streams, s_rows = x_sample.shape[:2]
    y_s, sgu_s, pool_s = _trunk(x_sample, mod_s, cache_pool, weights,
                                streams=n_streams, sgu_rows=s_rows, mlp_rows=s_rows)
    return (y_p, y_s, jnp.stack(pool_p, axis=0), jnp.stack(sgu_s, axis=0),
            jnp.stack(pool_s, axis=0))
```

```python
import functools

import jax
import jax.numpy as jnp
from jax import lax
from jax.experimental import pallas as pl
from jax.experimental.pallas import tpu as pltpu

D_MODEL = 1024
DEPTH = 4
N_MOD = 6
EPS = 1e-6
D_SGU = 2 * D_MODEL
SGU_GROUPS = 8
SGU_GROUP_DIM = D_SGU // SGU_GROUPS
SGU_LEN = 128
SGU_PAIR = 2
CHUNK = 64
POOL_WINDOWS = (2, 4, 8, 16)
POOL_GROUP_DIM = D_MODEL // len(POOL_WINDOWS)
POOL_HALO = max(POOL_WINDOWS)
POOL_STATE = POOL_HALO - 1
D_FF = 4 * D_MODEL

SGU_TILE_TOKENS = 512
MLP_TILE_TOKENS = 1024
FF_CHUNK = 512
POOL_SLICES = 8
POOL_EARLY_CHUNKS = 3
MOD_COLS = 1536
MOD_ROWS = 32
VMEM_LIMIT_BYTES = 56 * 1024 * 1024

_F32 = jnp.float32
_BF16 = jnp.bfloat16


def _dot(a, b):
    return jnp.dot(a, b, preferred_element_type=_F32)


def _rms(x, gain, gain_first=True):
    inv = lax.rsqrt(jnp.mean(x * x, axis=-1, keepdims=True) + EPS)
    return (x * gain) * inv if gain_first else x * inv * gain


def _layer_spec(stacked, layer):
    index = (layer,) + (0,) * (stacked.ndim - 1)
    return pl.BlockSpec((None,) + stacked.shape[1:], lambda *_: index,
                        pipeline_mode=pl.Buffered(1))


def _gelu_tanh(x):
    c = 0.7978845608028654
    half = 0.5 * x
    return half + half * jnp.tanh(x * (c + (c * 0.044715) * (x * x)))


def _mod_kernel(c_ref, w_ref, b_ref, o_ref):
    c = c_ref[...]
    s = (c * jax.nn.sigmoid(c)).astype(_BF16)
    o_ref[0] = _dot(s, w_ref[0].astype(_BF16)) + b_ref[0]


def _modulation(c_all, w_ada, b_ada):
    n_col = N_MOD * D_MODEL
    return pl.pallas_call(
        _mod_kernel,
        grid=(DEPTH, n_col // MOD_COLS),
        in_specs=[
            pl.BlockSpec((MOD_ROWS, D_MODEL), lambda i, j: (0, 0)),
            pl.BlockSpec((1, D_MODEL, MOD_COLS), lambda i, j: (i, 0, j)),
            pl.BlockSpec((1, 1, MOD_COLS), lambda i, j: (i, 0, j)),
        ],
        out_specs=pl.BlockSpec((1, MOD_ROWS, MOD_COLS), lambda i, j: (i, 0, j)),
        out_shape=jax.ShapeDtypeStruct((DEPTH, MOD_ROWS, n_col), _F32),
        compiler_params=pltpu.CompilerParams(
            dimension_semantics=("arbitrary", "arbitrary"),
            vmem_limit_bytes=VMEM_LIMIT_BYTES),
        name="adaln_mod",
    )(c_all, w_ada, b_ada.reshape(DEPTH, 1, n_col))


def _sgu_kernel(x_ref, mod_ref, g_ref, w_in_ref, gv_ref, ws_ref, mask_ref, bs_ref,
                w_out_ref, o_ref, *v_ref, streams, rows):
    m = streams * rows
    x = x_ref[...]
    mod = mod_ref[...]
    sh1, sc1, gt1 = mod[:, 0:1], mod[:, 1:2], mod[:, 2:3]
    g = g_ref[...]
    h = _rms(x, g[0:1], gain_first=False) * (1.0 + sc1) + sh1
    hb = h.reshape(m, D_MODEL).astype(_BF16)

    n_pairs = SGU_GROUPS // SGU_PAIR
    pair_cols = [slice(p * SGU_PAIR * SGU_GROUP_DIM, (p + 1) * SGU_PAIR * SGU_GROUP_DIM)
                 for p in range(n_pairs)]
    v_parts, u_parts = [], []
    for cols in pair_cols:
        vcols = slice(D_SGU + cols.start, D_SGU + cols.stop)
        v_parts.append(_gelu_tanh(_dot(hb, w_in_ref[:, vcols])))
        u_parts.append(_gelu_tanh(_dot(hb, w_in_ref[:, cols])))
    v = _rms(jnp.concatenate(v_parts, axis=1), gv_ref[...], gain_first=False)
    if v_ref:
        v_ref[0][...] = v.reshape(streams, rows, D_SGU)
    vb = v.astype(_BF16)

    mask = mask_ref[...]
    n_chunks = m // SGU_LEN
    acc = jnp.zeros((m, D_MODEL), _F32)
    for pair in range(n_pairs):
        pcols = pair_cols[pair]
        u = u_parts[pair]
        s_pair = []
        for grp in range(pair * SGU_PAIR, (pair + 1) * SGU_PAIR):
            cols = slice(grp * SGU_GROUP_DIM, (grp + 1) * SGU_GROUP_DIM)
            w = (ws_ref[grp] * mask).astype(_BF16)
            v_wide = jnp.concatenate(
                [vb[n * SGU_LEN:(n + 1) * SGU_LEN, cols] for n in range(n_chunks)], axis=1)
            s_wide = _dot(w, v_wide) + bs_ref[grp]
            s_pair.append(jnp.concatenate(
                [s_wide[:, n * SGU_GROUP_DIM:(n + 1) * SGU_GROUP_DIM]
                 for n in range(n_chunks)], axis=0))
        s = jnp.concatenate(s_pair, axis=1)
        acc = acc + _dot((u * s).astype(_BF16), w_out_ref[pcols, :])

    mix = _rms(acc, g[1:2], gain_first=False).reshape(streams, rows, D_MODEL)
    o_ref[...] = x + gt1 * mix


def _sgu_layer(x, mod, g_norm, w_in, g_v, w_s, mask, b_s, w_out, *, layer, streams, rows,
               emit_v):
    bsz, seq, _ = x.shape
    grid = (bsz // streams, seq // rows)
    tile = lambda d: pl.BlockSpec((streams, rows, d), lambda b, t: (b, t, 0))
    full = lambda a: pl.BlockSpec(a.shape, lambda b, t: (0,) * a.ndim)
    out_shape = [jax.ShapeDtypeStruct(x.shape, _F32)]
    out_specs = [tile(D_MODEL)]
    if emit_v:
        out_shape.append(jax.ShapeDtypeStruct((bsz, seq, D_SGU), _F32))
        out_specs.append(tile(D_SGU))
    return pl.pallas_call(
        functools.partial(_sgu_kernel, streams=streams, rows=rows),
        grid=grid,
        in_specs=[
            tile(D_MODEL),
            pl.BlockSpec((streams, N_MOD, D_MODEL), lambda b, t: (b, 0, 0)),
            full(g_norm), _layer_spec(w_in, layer), full(g_v), full(w_s), full(mask),
            full(b_s), _layer_spec(w_out, layer),
        ],
        out_specs=out_specs,
        out_shape=out_shape,
        compiler_params=pltpu.CompilerParams(
            dimension_semantics=("arbitrary", "arbitrary"),
            vmem_limit_bytes=VMEM_LIMIT_BYTES),
        name="sgu_mixer",
    )(x, mod, g_norm, w_in, g_v, w_s, mask, b_s, w_out)


def _ffn_pre(x, mod, g):
    streams, rows, _ = x.shape
    sh2, sc2 = mod[:, 3:4], mod[:, 4:5]
    h = _rms(x, g[2:3] * (1.0 + sc2)) + sh2
    return h.reshape(streams * rows, D_MODEL).astype(_BF16)


def _ffn_up(hb, w_up_ref, c):
    a = jnp.maximum(_dot(hb, w_up_ref[:, c * FF_CHUNK:(c + 1) * FF_CHUNK]), 0.0)
    return (a * a).astype(_BF16)


def _ffn_post(x, mod, g, hb, early, w_up_ref, w_down_ref):
    streams, rows, _ = x.shape
    acc = jnp.zeros((streams * rows, D_MODEL), _F32)
    for c in range(D_FF // FF_CHUNK):
        a = early[c] if c < len(early) else _ffn_up(hb, w_up_ref, c)
        acc = acc + _dot(a, w_down_ref[c * FF_CHUNK:(c + 1) * FF_CHUNK, :])
    return x + _rms(acc.reshape(streams, rows, D_MODEL), g[3:4] * mod[:, 5:6])


def _ffn_body(x, mod, g, w_up_ref, w_down_ref):
    return _ffn_post(x, mod, g, _ffn_pre(x, mod, g), [], w_up_ref, w_down_ref)


def _ffn_kernel(x_ref, mod_ref, g_ref, w_up_ref, w_down_ref, o_ref):
    o_ref[...] = _ffn_body(x_ref[...], mod_ref[...], g_ref[...], w_up_ref, w_down_ref)


def _ffn_layer(x, mod, g_norm, w_up, w_down, *, layer, streams, rows):
    bsz, seq, _ = x.shape
    grid = (bsz // streams, seq // rows)
    tile = pl.BlockSpec((streams, rows, D_MODEL), lambda b, t: (b, t, 0))
    full = lambda a: pl.BlockSpec(a.shape, lambda b, t: (0,) * a.ndim)
    return pl.pallas_call(
        _ffn_kernel,
        grid=grid,
        in_specs=[
            tile,
            pl.BlockSpec((streams, N_MOD, D_MODEL), lambda b, t: (b, 0, 0)),
            full(g_norm), _layer_spec(w_up, layer), _layer_spec(w_down, layer),
        ],
        out_specs=tile,
        out_shape=jax.ShapeDtypeStruct(x.shape, _F32),
        compiler_params=pltpu.CompilerParams(
            dimension_semantics=("arbitrary", "arbitrary"),
            vmem_limit_bytes=VMEM_LIMIT_BYTES),
        name="channel_mlp",
    )(x, mod, g_norm, w_up, w_down)


def _window_sums(ext, win):
    tot, span = ext, 1
    while span < win:
        tot = tot + pltpu.roll(tot, span, axis=1)
        span *= 2
    return tot


def _pool_ffn_kernel(*refs, hist, halo_is_h):
    if halo_is_h:
        (x_ref, halo_ref, mod_ref, g_ref, wg_ref, scale_ref, w_up_ref, w_down_ref,
         o_ref, tail_ref) = refs
    else:
        (x_ref, mod_ref, g_ref, wg_ref, scale_ref, w_up_ref, w_down_ref,
         o_ref, tail_ref, carry_ref) = refs
    t = pl.program_id(1)
    streams, rows, _ = x_ref.shape
    mod = mod_ref[...]
    sh1, sc1, gt1 = mod[:, 0:1], mod[:, 1:2], mod[:, 2:3]
    g = g_ref[...]
    if halo_is_h:
        halo = halo_ref[...]
    else:
        @pl.when(t == 0)
        def _():
            carry_ref[...] = jnp.zeros_like(carry_ref)
        halo = carry_ref[...]

    n_slices = POOL_SLICES if streams == 1 and rows % (POOL_SLICES * POOL_HALO) == 0 else 1
    n_early = POOL_EARLY_CHUNKS if n_slices > 1 else 0
    sl_rows = rows // n_slices
    x1_parts, hb_parts, early = [], [], [[] for _ in range(n_early)]
    for r in range(n_slices):
        x = x_ref[:, r * sl_rows:(r + 1) * sl_rows, :]
        h = _rms(x, g[0:1] * (1.0 + sc1)) + sh1
        ext = jnp.concatenate([halo, h], axis=1)
        halo = ext[:, sl_rows:, :]
        pos = (lax.broadcasted_iota(jnp.int32, (1, sl_rows, 1), 1)
               + (t * rows + r * sl_rows + hist))
        outs = []
        for grp, win in enumerate(POOL_WINDOWS):
            cols = slice(grp * POOL_GROUP_DIM, (grp + 1) * POOL_GROUP_DIM)
            tot = _window_sums(ext[:, :, cols], win)[:, POOL_HALO:, :]
            inv_cnt = 1.0 / jnp.minimum(win, pos + 1).astype(_F32)
            pooled = (tot * inv_cnt - h[:, :, cols]).reshape(streams * sl_rows, POOL_GROUP_DIM)
            outs.append(_dot(pooled.astype(_BF16), wg_ref[grp]))
        y = jnp.concatenate(outs, axis=-1) * scale_ref[...]
        x1 = x + _rms(y.reshape(streams, sl_rows, D_MODEL), g[1:2] * gt1)
        hb = _ffn_pre(x1, mod, g)
        if r > 0:
            for c in range(n_early):
                early[c].append(_ffn_up(hb_parts[r - 1], w_up_ref, c))
        x1_parts.append(x1)
        hb_parts.append(hb)
    for c in range(n_early):
        early[c].append(_ffn_up(hb_parts[-1], w_up_ref, c))
    tail_ref[...] = halo
    if not halo_is_h:
        carry_ref[...] = halo
    x1 = jnp.concatenate(x1_parts, axis=1)
    hb = jnp.concatenate(hb_parts, axis=0)
    early = [jnp.concatenate(parts, axis=0) for parts in early]
    o_ref[...] = _ffn_post(x1, mod, g, hb, early, w_up_ref, w_down_ref)


def _pool_ffn_layer(x, halo, mod, g_norm, w_grp, scale, w_up, w_down, *, layer, streams,
                    rows, hist):
    bsz, seq, _ = x.shape
    grid = (bsz // streams, seq // rows)
    halo_is_h = halo is not None
    tile = pl.BlockSpec((streams, rows, D_MODEL), lambda b, t: (b, t, 0))
    per_stream = lambda n: pl.BlockSpec((streams, n, D_MODEL), lambda b, t: (b, 0, 0))
    full = lambda a: pl.BlockSpec(a.shape, lambda b, t: (0,) * a.ndim)
    operands = [x] + ([halo] if halo_is_h else []) + [mod, g_norm, w_grp, scale, w_up, w_down]
    in_specs = ([tile] + ([per_stream(POOL_HALO)] if halo_is_h else [])
                + [per_stream(N_MOD), full(g_norm), full(w_grp), full(scale),
                   _layer_spec(w_up, layer), _layer_spec(w_down, layer)])
    scratch = [] if halo_is_h else [pltpu.VMEM((streams, POOL_HALO, D_MODEL), _F32)]
    return pl.pallas_call(
        functools.partial(_pool_ffn_kernel, hist=hist, halo_is_h=halo_is_h),
        grid=grid,
        in_specs=in_specs,
        out_specs=[tile, per_stream(POOL_HALO)],
        out_shape=[jax.ShapeDtypeStruct(x.shape, _F32),
                   jax.ShapeDtypeStruct((bsz, POOL_HALO, D_MODEL), _F32)],
        scratch_shapes=scratch,
        compiler_params=pltpu.CompilerParams(
            dimension_semantics=("arbitrary", "arbitrary"),
            vmem_limit_bytes=VMEM_LIMIT_BYTES),
        name="pool_mlp",
    )(*operands)


def _sgu_mask(length):
    pos = jnp.arange(SGU_LEN)
    same_chunk = (pos[:, None] // length) == (pos[None, :] // length)
    causal = (pos[None, :] % length) // CHUNK <= (pos[:, None] % length) // CHUNK
    return (same_chunk & causal).astype(_F32)


def _trunk(x, mod, pool_cache, weights, *, streams, sgu_rows, mlp_rows):
    (g_norm, sgu_w_in, sgu_g_v, sgu_w_s, sgu_b_s, sgu_w_out, pool_w_grp, pool_scale,
     ffn_w_up, ffn_w_down) = weights
    sample = pool_cache is not None
    seq = x.shape[1]
    length = seq if sample else SGU_LEN
    reps = SGU_LEN // length
    mask = _sgu_mask(length)
    sgu_states, pool_states = [], []
    for i in range(DEPTH):
        j = i // 2
        if i % 2 == 0:
            w_s = jnp.tile(sgu_w_s[j][:, :length, :length], (1, reps, reps))
            b_s = jnp.tile(sgu_b_s[j][:, :length], (1, reps))[:, :, None]
            outs = _sgu_layer(x, mod[i], g_norm[i], sgu_w_in, sgu_g_v[j][None, :],
                              w_s, mask, b_s, sgu_w_out, layer=j, streams=streams,
                              rows=sgu_rows, emit_v=sample)
            x = outs[0]
            if sample:
                sgu_states.append(outs[1])
            x = _ffn_layer(x, mod[i], g_norm[i], ffn_w_up, ffn_w_down, layer=i,
                           streams=streams, rows=mlp_rows)
        else:
            halo = jnp.pad(pool_cache[j], ((0, 0), (1, 0), (0, 0))) if sample else None
            x, tail = _pool_ffn_layer(
                x, halo, mod[i], g_norm[i], pool_w_grp[j], pool_scale[j][None, :],
                ffn_w_up, ffn_w_down, layer=i, streams=streams, rows=mlp_rows,
                hist=POOL_STATE if sample else 0)
            pool_states.append(tail[:, 1:, :])
    return x, sgu_states, pool_states


def kernel(x_prompt, x_sample, cache_pool, c_prompt, c_sample, g_norm, w_ada, b_ada,
           sgu_w_in, sgu_g_v, sgu_w_s, sgu_b_s, sgu_w_out, pool_w_grp, pool_scale,
           ffn_w_up, ffn_w_down):
    n_p, n_s = c_prompt.shape[0], c_sample.shape[0]
    c_all = jnp.concatenate(
        [c_prompt, c_sample, jnp.zeros((MOD_ROWS - n_p - n_s, D_MODEL), _F32)], axis=0)
    mod = _modulation(c_all, w_ada, b_ada)
    mod = mod.reshape(DEPTH, MOD_ROWS, N_MOD, D_MODEL)
    mod_p, mod_s = mod[:, :n_p], mod[:, n_p:n_p + n_s]

    weights = (g_norm, sgu_w_in.astype(_BF16), sgu_g_v, sgu_w_s, sgu_b_s,
               sgu_w_out.astype(_BF16), pool_w_grp.astype(_BF16), pool_scale,
               ffn_w_up.astype(_BF16), ffn_w_down.astype(_BF16))

    y_p, _, pool_p = _trunk(x_prompt, mod_p, None, weights,
                            streams=1, sgu_rows=SGU_TILE_TOKENS, mlp_rows=MLP_TILE_TOKENS)
    n_streams, s_rows = x_sample.shape[:2]
    y_s, sgu_s, pool_s = _trunk(x_sample, mod_s, cache_pool, weights,
                                streams=n_streams, sgu_rows=s_rows, mlp_rows=s_rows)
    return (y_p, y_s, jnp.stack(pool_p, axis=0), jnp.stack(sgu_s, axis=0),
            jnp.stack(pool_s, axis=0))
```

```python
import functools

import jax
import jax.numpy as jnp
from jax import lax
from jax.experimental import pallas as pl
from jax.experimental.pallas import tpu as pltpu

D_MODEL = 1024
DEPTH = 4
N_MOD = 6
EPS = 1e-6
D_SGU = 2 * D_MODEL
SGU_GROUPS = 8
SGU_GROUP_DIM = D_SGU // SGU_GROUPS
SGU_LEN = 128
SGU_PAIR = 2
CHUNK = 64
POOL_WINDOWS = (2, 4, 8, 16)
POOL_GROUP_DIM = D_MODEL // len(POOL_WINDOWS)
POOL_HALO = max(POOL_WINDOWS)
POOL_STATE = POOL_HALO - 1
D_FF = 4 * D_MODEL

SGU_TILE_TOKENS = 512
MLP_TILE_TOKENS = 1024
FF_CHUNK = 512
POOL_SLICES = 4
POOL_EARLY_CHUNKS = 4
MOD_COLS = 3072
MOD_ROWS = 32
VMEM_LIMIT_BYTES = 56 * 1024 * 1024

_F32 = jnp.float32
_BF16 = jnp.bfloat16


def _dot(a, b):
    return jnp.dot(a, b, preferred_element_type=_F32)


def _rms(x, gain, gain_first=True):
    inv = lax.rsqrt(jnp.mean(x * x, axis=-1, keepdims=True) + EPS)
    return (x * gain) * inv if gain_first else x * inv * gain


def _layer_spec(stacked, layer):
    index = (layer,) + (0,) * (stacked.ndim - 1)
    return pl.BlockSpec((None,) + stacked.shape[1:], lambda *_: index,
                        pipeline_mode=pl.Buffered(1))


def _gelu_tanh(x):
    c = 0.7978845608028654
    half = 0.5 * x
    return half + half * jnp.tanh(x * (c + (c * 0.044715) * (x * x)))


def _mod_kernel(c_ref, w_ref, b_ref, o_ref):
    c = c_ref[...]
    s = (c * jax.nn.sigmoid(c)).astype(_BF16)
    o_ref[0] = _dot(s, w_ref[0].astype(_BF16)) + b_ref[0]


def _modulation(c_all, w_ada, b_ada):
    n_col = N_MOD * D_MODEL
    return pl.pallas_call(
        _mod_kernel,
        grid=(DEPTH, n_col // MOD_COLS),
        in_specs=[
            pl.BlockSpec((MOD_ROWS, D_MODEL), lambda i, j: (0, 0)),
            pl.BlockSpec((1, D_MODEL, MOD_COLS), lambda i, j: (i, 0, j)),
            pl.BlockSpec((1, 1, MOD_COLS), lambda i, j: (i, 0, j)),
        ],
        out_specs=pl.BlockSpec((1, MOD_ROWS, MOD_COLS), lambda i, j: (i, 0, j)),
        out_shape=jax.ShapeDtypeStruct((DEPTH, MOD_ROWS, n_col), _F32),
        compiler_params=pltpu.CompilerParams(
            dimension_semantics=("arbitrary", "arbitrary"),
            vmem_limit_bytes=VMEM_LIMIT_BYTES),
        name="adaln_mod",
    )(c_all, w_ada, b_ada.reshape(DEPTH, 1, n_col))


def _sgu_kernel(x_ref, mod_ref, g_ref, w_in_ref, gv_ref, ws_ref, mask_ref, bs_ref,
                w_out_ref, o_ref, *v_ref, streams, rows):
    m = streams * rows
    x = x_ref[...]
    mod = mod_ref[...]
    sh1, sc1, gt1 = mod[:, 0:1], mod[:, 1:2], mod[:, 2:3]
    g = g_ref[...]
    h = _rms(x, g[0:1], gain_first=False) * (1.0 + sc1) + sh1
    hb = h.reshape(m, D_MODEL).astype(_BF16)

    n_pairs = SGU_GROUPS // SGU_PAIR
    pair_cols = [slice(p * SGU_PAIR * SGU_GROUP_DIM, (p + 1) * SGU_PAIR * SGU_GROUP_DIM)
                 for p in range(n_pairs)]
    v_parts, u_parts = [], []
    for cols in pair_cols:
        vcols = slice(D_SGU + cols.start, D_SGU + cols.stop)
        v_parts.append(_gelu_tanh(_dot(hb, w_in_ref[:, vcols])))
        u_parts.append(_gelu_tanh(_dot(hb, w_in_ref[:, cols])))
    v = _rms(jnp.concatenate(v_parts, axis=1), gv_ref[...], gain_first=False)
    if v_ref:
        v_ref[0][...] = v.reshape(streams, rows, D_SGU)
    vb = v.astype(_BF16)

    mask = mask_ref[...]
    n_chunks = m // SGU_LEN
    acc = jnp.zeros((m, D_MODEL), _F32)
    for pair in range(n_pairs):
        pcols = pair_cols[pair]
        u = u_parts[pair]
        s_pair = []
        for grp in range(pair * SGU_PAIR, (pair + 1) * SGU_PAIR):
            cols = slice(grp * SGU_GROUP_DIM, (grp + 1) * SGU_GROUP_DIM)
            w = (ws_ref[grp] * mask).astype(_BF16)
            v_wide = jnp.concatenate(
                [vb[n * SGU_LEN:(n + 1) * SGU_LEN, cols] for n in range(n_chunks)], axis=1)
            s_wide = _dot(w, v_wide) + bs_ref[grp]
            s_pair.append(jnp.concatenate(
                [s_wide[:, n * SGU_GROUP_DIM:(n + 1) * SGU_GROUP_DIM]
                 for n in range(n_chunks)], axis=0))
        s = jnp.concatenate(s_pair, axis=1)
        acc = acc + _dot((u * s).astype(_BF16), w_out_ref[pcols, :])

    mix = _rms(acc, g[1:2], gain_first=False).reshape(streams, rows, D_MODEL)
    o_ref[...] = x + gt1 * mix


def _sgu_layer(x, mod, g_norm, w_in, g_v, w_s, mask, b_s, w_out, *, layer, streams, rows,
               emit_v):
    bsz, seq, _ = x.shape
    grid = (bsz // streams, seq // rows)
    tile = lambda d: pl.BlockSpec((streams, rows, d), lambda b, t: (b, t, 0))
    full = lambda a: pl.BlockSpec(a.shape, lambda b, t: (0,) * a.ndim)
    out_shape = [jax.ShapeDtypeStruct(x.shape, _F32)]
    out_specs = [tile(D_MODEL)]
    if emit_v:
        out_shape.append(jax.ShapeDtypeStruct((bsz, seq, D_SGU), _F32))
        out_specs.append(tile(D_SGU))
    return pl.pallas_call(
        functools.partial(_sgu_kernel, streams=streams, rows=rows),
        grid=grid,
        in_specs=[
            tile(D_MODEL),
            pl.BlockSpec((streams, N_MOD, D_MODEL), lambda b, t: (b, 0, 0)),
            full(g_norm), _layer_spec(w_in, layer), full(g_v), full(w_s), full(mask),
            full(b_s), _layer_spec(w_out, layer),
        ],
        out_specs=out_specs,
        out_shape=out_shape,
        compiler_params=pltpu.CompilerParams(
            dimension_semantics=("arbitrary", "arbitrary"),
            vmem_limit_bytes=VMEM_LIMIT_BYTES),
        name="sgu_mixer",
    )(x, mod, g_norm, w_in, g_v, w_s, mask, b_s, w_out)


def _ffn_pre(x, mod, g):
    streams, rows, _ = x.shape
    sh2, sc2 = mod[:, 3:4], mod[:, 4:5]
    h = _rms(x, g[2:3] * (1.0 + sc2)) + sh2
    return h.reshape(streams * rows, D_MODEL).astype(_BF16)


def _ffn_up(hb, w_up_ref, c):
    a = jnp.maximum(_dot(hb, w_up_ref[:, c * FF_CHUNK:(c + 1) * FF_CHUNK]), 0.0)
    return (a * a).astype(_BF16)


def _ffn_post(x, mod, g, hb, early, w_up_ref, w_down_ref):
    streams, rows, _ = x.shape
    acc = jnp.zeros((streams * rows, D_MODEL), _F32)
    for c in range(D_FF // FF_CHUNK):
        a = early[c] if c < len(early) else _ffn_up(hb, w_up_ref, c)
        acc = acc + _dot(a, w_down_ref[c * FF_CHUNK:(c + 1) * FF_CHUNK, :])
    return x + _rms(acc.reshape(streams, rows, D_MODEL), g[3:4] * mod[:, 5:6])


def _ffn_body(x, mod, g, w_up_ref, w_down_ref):
    return _ffn_post(x, mod, g, _ffn_pre(x, mod, g), [], w_up_ref, w_down_ref)


def _ffn_kernel(x_ref, mod_ref, g_ref, w_up_ref, w_down_ref, o_ref):
    o_ref[...] = _ffn_body(x_ref[...], mod_ref[...], g_ref[...], w_up_ref, w_down_ref)


def _ffn_layer(x, mod, g_norm, w_up, w_down, *, layer, streams, rows):
    bsz, seq, _ = x.shape
    grid = (bsz // streams, seq // rows)
    tile = pl.BlockSpec((streams, rows, D_MODEL), lambda b, t: (b, t, 0))
    full = lambda a: pl.BlockSpec(a.shape, lambda b, t: (0,) * a.ndim)
    return pl.pallas_call(
        _ffn_kernel,
        grid=grid,
        in_specs=[
            tile,
            pl.BlockSpec((streams, N_MOD, D_MODEL), lambda b, t: (b, 0, 0)),
            full(g_norm), _layer_spec(w_up, layer), _layer_spec(w_down, layer),
        ],
        out_specs=tile,
        out_shape=jax.ShapeDtypeStruct(x.shape, _F32),
        compiler_params=pltpu.CompilerParams(
            dimension_semantics=("arbitrary", "arbitrary"),
            vmem_limit_bytes=VMEM_LIMIT_BYTES),
        name="channel_mlp",
    )(x, mod, g_norm, w_up, w_down)


def _window_sums(ext, win):
    tot, span = ext, 1
    while span < win:
        tot = tot + pltpu.roll(tot, span, axis=1)
        span *= 2
    return tot


def _pool_ffn_kernel(*refs, hist, halo_is_h):
    if halo_is_h:
        (x_ref, halo_ref, mod_ref, g_ref, wg_ref, scale_ref, w_up_ref, w_down_ref,
         o_ref, tail_ref) = refs
    else:
        (x_ref, mod_ref, g_ref, wg_ref, scale_ref, w_up_ref, w_down_ref,
         o_ref, tail_ref, carry_ref) = refs
    t = pl.program_id(1)
    streams, rows, _ = x_ref.shape
    mod = mod_ref[...]
    sh1, sc1, gt1 = mod[:, 0:1], mod[:, 1:2], mod[:, 2:3]
    g = g_ref[...]
    if halo_is_h:
        halo = halo_ref[...]
    else:
        @pl.when(t == 0)
        def _():
            carry_ref[...] = jnp.zeros_like(carry_ref)
        halo = carry_ref[...]

    n_slices = POOL_SLICES if streams == 1 and rows % (POOL_SLICES * POOL_HALO) == 0 else 1
    n_early = POOL_EARLY_CHUNKS if n_slices > 1 else 0
    sl_rows = rows // n_slices
    x1_parts, hb_parts, early = [], [], [[] for _ in range(n_early)]
    for r in range(n_slices):
        x = x_ref[:, r * sl_rows:(r + 1) * sl_rows, :]
        h = _rms(x, g[0:1] * (1.0 + sc1)) + sh1
        ext = jnp.concatenate([halo, h], axis=1)
        halo = ext[:, sl_rows:, :]
        pos = (lax.broadcasted_iota(jnp.int32, (1, sl_rows, 1), 1)
               + (t * rows + r * sl_rows + hist))
        outs = []
        for grp, win in enumerate(POOL_WINDOWS):
            cols = slice(grp * POOL_GROUP_DIM, (grp + 1) * POOL_GROUP_DIM)
            tot = _window_sums(ext[:, :, cols], win)[:, POOL_HALO:, :]
            inv_cnt = 1.0 / jnp.minimum(win, pos + 1).astype(_F32)
            pooled = (tot * inv_cnt - h[:, :, cols]).reshape(streams * sl_rows, POOL_GROUP_DIM)
            outs.append(_dot(pooled.astype(_BF16), wg_ref[grp]))
        y = jnp.concatenate(outs, axis=-1) * scale_ref[...]
        x1 = x + _rms(y.reshape(streams, sl_rows, D_MODEL), g[1:2] * gt1)
        hb = _ffn_pre(x1, mod, g)
        if r > 0:
            for c in range(n_early):
                early[c].append(_ffn_up(hb_parts[r - 1], w_up_ref, c))
        x1_parts.append(x1)
        hb_parts.append(hb)
    for c in range(n_early):
        early[c].append(_ffn_up(hb_parts[-1], w_up_ref, c))
    tail_ref[...] = halo
    if not halo_is_h:
        carry_ref[...] = halo
    x1 = jnp.concatenate(x1_parts, axis=1)
    hb = jnp.concatenate(hb_parts, axis=0)
    early = [jnp.concatenate(parts, axis=0) for parts in early]
    o_ref[...] = _ffn_post(x1, mod, g, hb, early, w_up_ref, w_down_ref)


def _pool_ffn_layer(x, halo, mod, g_norm, w_grp, scale, w_up, w_down, *, layer, streams,
                    rows, hist):
    bsz, seq, _ = x.shape
    grid = (bsz // streams, seq // rows)
    halo_is_h = halo is not None
    tile = pl.BlockSpec((streams, rows, D_MODEL), lambda b, t: (b, t, 0))
    per_stream = lambda n: pl.BlockSpec((streams, n, D_MODEL), lambda b, t: (b, 0, 0))
    full = lambda a: pl.BlockSpec(a.shape, lambda b, t: (0,) * a.ndim)
    operands = [x] + ([halo] if halo_is_h else []) + [mod, g_norm, w_grp, scale, w_up, w_down]
    in_specs = ([tile] + ([per_stream(POOL_HALO)] if halo_is_h else [])
                + [per_stream(N_MOD), full(g_norm), full(w_grp), full(scale),
                   _layer_spec(w_up, layer), _layer_spec(w_down, layer)])
    scratch = [] if halo_is_h else [pltpu.VMEM((streams, POOL_HALO, D_MODEL), _F32)]
    return pl.pallas_call(
        functools.partial(_pool_ffn_kernel, hist=hist, halo_is_h=halo_is_h),
        grid=grid,
        in_specs=in_specs,
        out_specs=[tile, per_stream(POOL_HALO)],
        out_shape=[jax.ShapeDtypeStruct(x.shape, _F32),
                   jax.ShapeDtypeStruct((bsz, POOL_HALO, D_MODEL), _F32)],
        scratch_shapes=scratch,
        compiler_params=pltpu.CompilerParams(
            dimension_semantics=("arbitrary", "arbitrary"),
            vmem_limit_bytes=VMEM_LIMIT_BYTES),
        name="pool_mlp",
    )(*operands)


def _sgu_mask(length):
    pos = jnp.arange(SGU_LEN)
    same_chunk = (pos[:, None] // length) == (pos[None, :] // length)
    causal = (pos[None, :] % length) // CHUNK <= (pos[:, None] % length) // CHUNK
    return (same_chunk & causal).astype(_F32)


def _trunk(x, mod, pool_cache, weights, *, streams, sgu_rows, mlp_rows):
    (g_norm, sgu_w_in, sgu_g_v, sgu_w_s, sgu_b_s, sgu_w_out, pool_w_grp, pool_scale,
     ffn_w_up, ffn_w_down) = weights
    sample = pool_cache is not None
    seq = x.shape[1]
    length = seq if sample else SGU_LEN
    reps = SGU_LEN // length
    mask = _sgu_mask(length)
    sgu_states, pool_states = [], []
    for i in range(DEPTH):
        j = i // 2
        if i % 2 == 0:
            w_s = jnp.tile(sgu_w_s[j][:, :length, :length], (1, reps, reps))
            b_s = jnp.tile(sgu_b_s[j][:, :length], (1, reps))[:, :, None]
            outs = _sgu_layer(x, mod[i], g_norm[i], sgu_w_in, sgu_g_v[j][None, :],
                              w_s, mask, b_s, sgu_w_out, layer=j, streams=streams,
                              rows=sgu_rows, emit_v=sample)
            x = outs[0]
            if sample:
                sgu_states.append(outs[1])
            x = _ffn_layer(x, mod[i], g_norm[i], ffn_w_up, ffn_w_down, layer=i,
                           streams=streams, rows=mlp_rows)
        else:
            halo = jnp.pad(pool_cache[j], ((0, 0), (1, 0), (0, 0))) if sample else None
            x, tail = _pool_ffn_layer(
                x, halo, mod[i], g_norm[i], pool_w_grp[j], pool_scale[j][None, :],
                ffn_w_up, ffn_w_down, layer=i, streams=streams, rows=mlp_rows,
                hist=POOL_STATE if sample else 0)
            pool_states.append(tail[:, 1:, :])
    return x, sgu_states, pool_states


def kernel(x_prompt, x_sample, cache_pool, c_prompt, c_sample, g_norm, w_ada, b_ada,
           sgu_w_in, sgu_g_v, sgu_w_s, sgu_b_s, sgu_w_out, pool_w_grp, pool_scale,
           ffn_w_up, ffn_w_down):
    n_p, n_s = c_prompt.shape[0], c_sample.shape[0]
    c_all = jnp.concatenate(
        [c_prompt, c_sample, jnp.zeros((MOD_ROWS - n_p - n_s, D_MODEL), _F32)], axis=0)
    mod = _modulation(c_all, w_ada, b_ada)
    mod = mod.reshape(DEPTH, MOD_ROWS, N_MOD, D_MODEL)
    mod_p, mod_s = mod[:, :n_p], mod[:, n_p:n_p + n_s]

    weights = (g_norm, sgu_w_in.astype(_BF16), sgu_g_v, sgu_w_s, sgu_b_s,
               sgu_w_out.astype(_BF16), pool_w_grp.astype(_BF16), pool_scale,
               ffn_w_up.astype(_BF16), ffn_w_down.astype(_BF16))

    y_p, _, pool_p = _trunk(x_prompt, mod_p, None, weights,
                            streams=1, sgu_rows=SGU_TILE_TOKENS, mlp_rows=MLP_TILE_TOKENS)
    n_streams, s_rows = x_sample.shape[:2]
    y_s, sgu_s, pool_s = _trunk(x_sample, mod_s, cache_pool, weights,
                                streams=n_streams, sgu_rows=s_rows, mlp_rows=s_rows)
    return (y_p, y_s, jnp.stack(pool_p, axis=0), jnp.stack(sgu_s, axis=0),
            jnp.stack(pool_s, axis=0))
```
